```python
import jax, jax.numpy as jnp
from jax import lax
import numpy as np

D_MODEL = 2048
BATCH = 16
SEQ = 2048
DEPTH = 2
DEC_BATCH = 16
DEC_SEQ = 16
PAST_LEN = 1024

CHUNK = 64
N_META = 16
C_CONV = D_MODEL
CONV_WIDTH = 31
CONV_HIST = CONV_WIDTH - 1
N_HEADS = 4
DK = D_MODEL // 2
DV = D_MODEL
DK_HEAD = DK // N_HEADS
DV_HEAD = DV // N_HEADS
GATE_RANK = 16
GATE_TEMP = 16.0
D_FF = 5632
N_EXPERTS = 8
TOP_K = 2
D_FF_EXPERT = 7168
MOE_BLOCK = 128
N_DENSE = (DEPTH + 1) // 2
N_MOE = DEPTH // 2
EPS = 1e-6
IN_SPLITS = (C_CONV, C_CONV, DK, DK, DV, DV, GATE_RANK, D_MODEL, D_MODEL)
D_IN = 2 * C_CONV + 2 * DK + 2 * DV + GATE_RANK + 2 * D_MODEL

kernel_name = 'conv_gla_gated_streaming_encoder'


def rmsnorm(x, g):
    xf = x.astype(jnp.float32)
    y = xf * lax.rsqrt(jnp.mean(xf * xf, axis=-1, keepdims=True) + EPS)
    return (y * g.astype(jnp.float32)).astype(x.dtype)


def layernorm(x, g, b):
    xf = x.astype(jnp.float32)
    mu = jnp.mean(xf, axis=-1, keepdims=True)
    xc = xf - mu
    y = xc * lax.rsqrt(jnp.mean(xc * xc, axis=-1, keepdims=True) + EPS)
    return (y * g.astype(jnp.float32) + b.astype(jnp.float32)).astype(x.dtype)


def causal_dwconv(hist, u, w, b):
    full = jnp.concatenate([hist.astype(u.dtype), u], axis=1)
    y = lax.conv_general_dilated(full, w[:, None, :].astype(u.dtype), window_strides=(1,),
                                 padding='VALID', dimension_numbers=('NWC', 'WIO', 'NWC'),
                                 feature_group_count=C_CONV)
    return y + b.astype(u.dtype), full[:, -CONV_HIST:]


def gla_block(S, blk):
    q, k, v, lg = blk
    b = jnp.cumsum(lg, axis=2)
    o = jnp.einsum('bhcd,bhde->bhce', q * jnp.exp(b), S)
    c = q.shape[2]
    causal = jnp.tril(jnp.ones((c, c), dtype=bool))
    diff = b[:, :, :, None, :] - b[:, :, None, :, :]
    decay = jnp.exp(jnp.where(causal[:, :, None], diff, -jnp.inf))
    a = jnp.einsum('bhid,bhjd,bhijd->bhij', q, k, decay)
    o = o + jnp.einsum('bhij,bhje->bhie', a, v)
    b_last = b[:, :, -1:, :]
    S = S * jnp.exp(b_last)[:, :, 0, :, None] + jnp.einsum('bhcd,bhce->bhde', k * jnp.exp(b_last - b), v)
    return S, o


def gla(q, k, v, lg, S0):
    B, T = q.shape[0], q.shape[1]
    pad = (-T) % CHUNK
    n = (T + pad) // CHUNK

    def prep(a):
        a = jnp.pad(a.astype(jnp.float32), ((0, 0), (pad, 0), (0, 0), (0, 0)))
        a = a.reshape(B, n, CHUNK, N_HEADS, a.shape[-1])
        return jnp.transpose(a, (1, 0, 3, 2, 4))

    S, o = lax.scan(gla_block, S0.astype(jnp.float32), (prep(q), prep(k), prep(v), prep(lg)))
    o = jnp.transpose(o, (1, 0, 3, 2, 4)).reshape(B, n * CHUNK, N_HEADS, DV_HEAD)[:, pad:]
    return o, S


def token_mix(xn, conv_hist, S0, w_in, w_a2, b_a, conv_w, conv_b, ln_g, ln_b,
              w_conv_out, gla_norm_g, w_gla_out, w_out):
    B, T, _ = xn.shape
    proj = xn @ w_in
    offs = [int(o) for o in np.cumsum(IN_SPLITS)[:-1]]
    u_a, u_b, q, k, v, g, a_lr, z_conv, z_gla = jnp.split(proj, offs, axis=-1)
    glu = u_a * jax.nn.sigmoid(u_b)
    c, new_hist = causal_dwconv(conv_hist, glu, conv_w, conv_b)
    c = jax.nn.silu(layernorm(c, ln_g, ln_b))
    conv_out = c @ w_conv_out
    lg = jax.nn.log_sigmoid((a_lr @ w_a2 + b_a).astype(jnp.float32)) / GATE_TEMP
    heads = lambda t, d: t.reshape(B, T, N_HEADS, d)
    o, S = gla(heads(q, DK_HEAD) * (DK_HEAD ** -0.5), heads(k, DK_HEAD), heads(v, DV_HEAD),
               heads(lg, DK_HEAD), S0)
    o = rmsnorm(o, gla_norm_g).astype(xn.dtype).reshape(B, T, DV) * jax.nn.silu(g)
    gla_out = o @ w_gla_out
    mixed = jax.nn.sigmoid(z_conv) * conv_out + jax.nn.sigmoid(z_gla) * gla_out
    return mixed @ w_out, new_hist, S.astype(S0.dtype)


def swiglu(h, w1, w3, w2):
    return (jax.nn.silu(h @ w1) * (h @ w3)) @ w2


def moe_swiglu(h, router_w, w1, w3, w2):
    B, T, D = h.shape
    x = h.reshape(-1, D)
    N = x.shape[0]
    rows = N * TOP_K
    logits = (x @ router_w).astype(jnp.float32)
    top_v, top_i = lax.top_k(logits, TOP_K)
    gate = jax.nn.softmax(top_v, axis=-1)
    flat_e = top_i.reshape(-1).astype(jnp.int32)
    order = jnp.argsort(flat_e)
    sorted_e = flat_e[order]
    tok = order // TOP_K
    sizes = jnp.bincount(flat_e, length=N_EXPERTS).astype(jnp.int32)
    starts = jnp.cumsum(sizes) - sizes
    padded = ((sizes + MOE_BLOCK - 1) // MOE_BLOCK) * MOE_BLOCK
    pad_ends = jnp.cumsum(padded)
    pad_starts = pad_ends - padded
    dest = pad_starts[sorted_e] + jnp.arange(rows, dtype=jnp.int32) - starts[sorted_e]
    n_blocks = -(-rows // MOE_BLOCK) + N_EXPERTS
    buf = jnp.zeros((n_blocks * MOE_BLOCK, D), x.dtype).at[dest].set(x[tok])
    block_starts = jnp.arange(n_blocks, dtype=jnp.int32) * MOE_BLOCK
    block_e = jnp.minimum(jnp.searchsorted(pad_ends, block_starts, side='right'), N_EXPERTS - 1)

    def expert_block(args):
        xb, e = args
        return swiglu(xb, w1[e], w3[e], w2[e])

    yb = lax.map(expert_block, (buf.reshape(n_blocks, MOE_BLOCK, D), block_e))
    y = yb.reshape(-1, D)[dest] * gate.reshape(-1)[order][:, None].astype(x.dtype)
    out = jnp.zeros_like(x).at[tok].add(y)
    return out.reshape(B, T, D)


def trunk(x, conv_hist, gla_state, p):
    (norm_mix_g, w_in, w_gate_a2, b_gate_a, conv_w, conv_b, conv_ln_g, conv_ln_b,
     w_conv_out, gla_norm_g, w_gla_out, w_out, norm_ffn_g, ffn_w1, ffn_w3, ffn_w2,
     router_w, exp_w1, exp_w3, exp_w2, final_norm_g) = p
    new_conv, new_gla = [], []
    for i in range(DEPTH):
        m, hbuf, s = token_mix(rmsnorm(x, norm_mix_g[i]), conv_hist[i], gla_state[i],
                               w_in[i], w_gate_a2[i], b_gate_a[i], conv_w[i], conv_b[i],
                               conv_ln_g[i], conv_ln_b[i], w_conv_out[i], gla_norm_g[i],
                               w_gla_out[i], w_out[i])
        x = x + m
        h = rmsnorm(x, norm_ffn_g[i])
        j = i // 2
        if i % 2 == 0:
            x = x + swiglu(h, ffn_w1[j], ffn_w3[j], ffn_w2[j])
        else:
            x = x + moe_swiglu(h, router_w[j], exp_w1[j], exp_w3[j], exp_w2[j])
        new_conv.append(hbuf)
        new_gla.append(s)
    return rmsnorm(x, final_norm_g), jnp.stack(new_conv), jnp.stack(new_gla)


def setup_inputs(seed: int = 0) -> dict:
    key = jax.random.key(seed)
    ks = jax.random.split(key, 32)
    nrm = lambda k, shape, s: jax.random.normal(k, shape, jnp.float32) * s
    return {
        'x_prompt': nrm(ks[0], (BATCH, SEQ, D_MODEL), 1.0),
        'x_sample': nrm(ks[1], (DEC_BATCH, DEC_SEQ, D_MODEL), 1.0),
        'cache_conv': nrm(ks[2], (DEPTH, DEC_BATCH, CONV_HIST, C_CONV), 0.5),
        'state_gla': nrm(ks[3], (DEPTH, DEC_BATCH, N_HEADS, DK_HEAD, DV_HEAD), 1.0),
        'meta_tokens': nrm(ks[4], (N_META, D_MODEL), 1.0),
        'norm_mix_g': 1.0 + nrm(ks[5], (DEPTH, D_MODEL), 0.02),
        'w_in': nrm(ks[6], (DEPTH, D_MODEL, D_IN), D_MODEL ** -0.5),
        'w_gate_a2': nrm(ks[7], (DEPTH, GATE_RANK, DK), GATE_RANK ** -0.5),
        'b_gate_a': nrm(ks[8], (DEPTH, DK), 0.1),
        'conv_w': nrm(ks[9], (DEPTH, CONV_WIDTH, C_CONV), CONV_WIDTH ** -0.5),
        'conv_b': nrm(ks[10], (DEPTH, C_CONV), 0.02),
        'conv_ln_g': 1.0 + nrm(ks[11], (DEPTH, C_CONV), 0.02),
        'conv_ln_b': nrm(ks[12], (DEPTH, C_CONV), 0.02),
        'w_conv_out': nrm(ks[13], (DEPTH, C_CONV, D_MODEL), C_CONV ** -0.5),
        'gla_norm_g': 1.0 + nrm(ks[14], (DEPTH, DV_HEAD), 0.02),
        'w_gla_out': nrm(ks[15], (DEPTH, DV, D_MODEL), DV ** -0.5),
        'w_out': nrm(ks[16], (DEPTH, D_MODEL, D_MODEL), D_MODEL ** -0.5),
        'norm_ffn_g': 1.0 + nrm(ks[17], (DEPTH, D_MODEL), 0.02),
        'ffn_w1': nrm(ks[18], (N_DENSE, D_MODEL, D_FF), D_MODEL ** -0.5),
        'ffn_w3': nrm(ks[19], (N_DENSE, D_MODEL, D_FF), D_MODEL ** -0.5),
        'ffn_w2': nrm(ks[20], (N_DENSE, D_FF, D_MODEL), D_FF ** -0.5),
        'router_w': nrm(ks[21], (N_MOE, D_MODEL, N_EXPERTS), D_MODEL ** -0.5),
        'exp_w1': nrm(ks[22], (N_MOE, N_EXPERTS, D_MODEL, D_FF_EXPERT), D_MODEL ** -0.5),
        'exp_w3': nrm(ks[23], (N_MOE, N_EXPERTS, D_MODEL, D_FF_EXPERT), D_MODEL ** -0.5),
        'exp_w2': nrm(ks[24], (N_MOE, N_EXPERTS, D_FF_EXPERT, D_MODEL), D_FF_EXPERT ** -0.5),
        'final_norm_g': 1.0 + nrm(ks[25], (D_MODEL,), 0.02),
    }


def reference(x_prompt, x_sample, cache_conv, state_gla, meta_tokens, norm_mix_g, w_in,
              w_gate_a2, b_gate_a, conv_w, conv_b, conv_ln_g, conv_ln_b, w_conv_out,
              gla_norm_g, w_gla_out, w_out, norm_ffn_g, ffn_w1, ffn_w3, ffn_w2,
              router_w, exp_w1, exp_w3, exp_w2, final_norm_g):
    p = (norm_mix_g, w_in, w_gate_a2, b_gate_a, conv_w, conv_b, conv_ln_g, conv_ln_b,
         w_conv_out, gla_norm_g, w_gla_out, w_out, norm_ffn_g, ffn_w1, ffn_w3, ffn_w2,
         router_w, exp_w1, exp_w3, exp_w2, final_norm_g)
    bp = x_prompt.shape[0]
    meta = jnp.broadcast_to(meta_tokens[None].astype(x_prompt.dtype), (bp, N_META, D_MODEL))
    xp = jnp.concatenate([meta, x_prompt], axis=1)
    zero_conv = jnp.zeros((DEPTH, bp, CONV_HIST, C_CONV), cache_conv.dtype)
    zero_gla = jnp.zeros((DEPTH, bp, N_HEADS, DK_HEAD, DV_HEAD), state_gla.dtype)
    yp, conv_p, gla_p = trunk(xp, zero_conv, zero_gla, p)
    ys, conv_s, gla_s = trunk(x_sample, cache_conv, state_gla, p)
    return (yp[:, N_META:], ys, conv_p, gla_p, conv_s, gla_s)
```

```python
import functools

import jax
import jax.numpy as jnp
from jax import lax
from jax.experimental import pallas as pl
from jax.experimental.pallas import tpu as pltpu

CHUNK = 64
SUB = 16
GATE_TEMP = 16.0
EPS = 1e-6
TOP_K = 2
LANES = 128
SUBLANES = 8
BF16_ROWS = 16
VMEM_LIMIT_BYTES = 56 * 1024 * 1024

F32 = jnp.float32
BF16 = jnp.bfloat16


def _pick_tile(n, target, mult):
    best = None
    for d in range(mult, min(n, target) + 1, mult):
        if n % d == 0:
            best = d
    assert best is not None, (n, target, mult)
    return best


def _params(sem):
    return pltpu.CompilerParams(dimension_semantics=sem, vmem_limit_bytes=VMEM_LIMIT_BYTES)


def _dot(a, b):
    return jnp.dot(a, b, preferred_element_type=F32)


def _split3(x):
    hi = x.astype(BF16)
    r1 = x - hi.astype(F32)
    mid = r1.astype(BF16)
    lo = (r1 - mid.astype(F32)).astype(BF16)
    return hi, mid, lo


def _dot_f32(a, b):
    a_hi, a_mid, _ = _split3(a)
    b_hi, b_mid, _ = _split3(b)
    return _dot(a_hi, b_hi) + _dot(a_mid, b_hi) + _dot(a_hi, b_mid)


def _rms(x, g):
    return x * lax.rsqrt(jnp.mean(x * x, axis=-1, keepdims=True) + EPS) * g


def _sigmoid(x):
    return jax.nn.sigmoid(x)


def _log_sigmoid(z):
    return jnp.minimum(z, 0.0) - jnp.log1p(jnp.exp(-jnp.abs(z)))


def _inproj_kernel(x_ref, g_ref, wm_ref, wlr_ref, wa2_ref, ba_ref, p_ref, lg_ref, nb_ref):
    @pl.when(pl.program_id(1) == 0)
    def _():
        nb = _rms(x_ref[...], g_ref[...]).astype(BF16)
        nb_ref[...] = nb
        a_lr = _dot(nb, wlr_ref[...])
        z = _dot_f32(a_lr, wa2_ref[...]) + ba_ref[...]
        lg_ref[...] = _log_sigmoid(z) * (1.0 / GATE_TEMP)

    p_ref[...] = _dot(nb_ref[...], wm_ref[...])


def _inproj(x, g, wm, wlr, wa2, ba, tm, tn):
    n, d = x.shape
    n_out = wm.shape[1]
    dk = wa2.shape[1]
    return pl.pallas_call(
        _inproj_kernel,
        grid=(n // tm, n_out // tn),
        in_specs=[
            pl.BlockSpec((tm, d), lambda i, j: (i, 0)),
            pl.BlockSpec((1, d), lambda i, j: (0, 0)),
            pl.BlockSpec((d, tn), lambda i, j: (0, j)),
            pl.BlockSpec((d, LANES), lambda i, j: (0, 0)),
            pl.BlockSpec((LANES, dk), lambda i, j: (0, 0)),
            pl.BlockSpec((1, dk), lambda i, j: (0, 0)),
        ],
        out_specs=[
            pl.BlockSpec((tm, tn), lambda i, j: (i, j)),
            pl.BlockSpec((tm, dk), lambda i, j: (i, 0)),
        ],
        out_shape=[jax.ShapeDtypeStruct((n, n_out), F32), jax.ShapeDtypeStruct((n, dk), F32)],
        scratch_shapes=[pltpu.VMEM((tm, d), BF16)],
        compiler_params=_params(("parallel", "arbitrary")),
        name="inproj",
    )(x, g, wm, wlr, wa2, ba)


def _conv_kernel(*refs, tt, width, hp, zero_init, lane_chunk, has_prev):
    if has_prev:
        refs = refs[1:]
    if zero_init:
        ua_ref, ub_ref, w_ref, cb_ref, lng_ref, lnb_ref, c_ref, nh_ref, buf_ref, y_ref = refs
        hist_ref = None
    else:
        ua_ref, ub_ref, hist_ref, w_ref, cb_ref, lng_ref, lnb_ref, c_ref, nh_ref, buf_ref, y_ref = refs
    t = pl.program_id(1)
    hist = width - 1
    off = hp - hist
    d = ua_ref.shape[1]

    @pl.when(t == 0)
    def _():
        buf_ref[0:hp, :] = jnp.zeros((hp, d), F32)
        if not zero_init:
            buf_ref[off:hp, :] = hist_ref[0]

    @pl.when(t > 0)
    def _():
        buf_ref[0:hp, :] = buf_ref[tt:tt + hp, :]

    buf_ref[hp:hp + tt, :] = ua_ref[...] * _sigmoid(ub_ref[...])

    def rows(i, carry):
        r0 = pl.multiple_of(i * SUBLANES, SUBLANES)
        win = buf_ref.at[pl.ds(r0, hp + SUBLANES)]
        for lc in range(d // lane_chunk):
            cols = slice(lc * lane_chunk, (lc + 1) * lane_chunk)
            acc = jnp.zeros((SUBLANES, lane_chunk), F32)
            for w in range(width):
                acc = acc + win[off + w:off + w + SUBLANES, cols] * w_ref[w, :, cols]
            y_ref[pl.ds(r0, SUBLANES), cols] = acc
        return carry

    lax.fori_loop(0, tt // SUBLANES, rows, 0)

    @pl.when(t == pl.num_programs(1) - 1)
    def _():
        nh_ref[0] = buf_ref[tt + off:tt + hp, :]

    y = y_ref[...] + cb_ref[...]
    mu = jnp.mean(y, axis=-1, keepdims=True)
    yc = y - mu
    c = yc * lax.rsqrt(jnp.mean(yc * yc, axis=-1, keepdims=True) + EPS) * lng_ref[...] + lnb_ref[...]
    c_ref[...] = (c * _sigmoid(c)).astype(BF16)


def _conv_branch(p, hist, w8, cb, lng, lnb, *, n, nb, t_len, row0, tt, c_dim, prev=None):
    width = w8.shape[0]
    hp = -(-(width - 1) // SUBLANES) * SUBLANES
    assert t_len % tt == 0 and row0 % tt == 0 and tt % BF16_ROWS == 0
    assert tt >= hp or t_len == tt
    nt = t_len // tt
    rb0 = row0 // tt
    zero_init = hist is None
    kern = functools.partial(_conv_kernel, tt=tt, width=width, hp=hp, zero_init=zero_init,
                             lane_chunk=min(c_dim, 4 * LANES), has_prev=prev is not None)
    vec = pl.BlockSpec((1, c_dim), lambda b, t: (0, 0))
    in_specs = [pl.BlockSpec(memory_space=pl.ANY)] if prev is not None else []
    args = [prev] if prev is not None else []
    in_specs += [
        pl.BlockSpec((tt, c_dim), lambda b, t: (rb0 + b * nt + t, 0)),
        pl.BlockSpec((tt, c_dim), lambda b, t: (rb0 + b * nt + t, 1)),
    ]
    args += [p, p]
    if not zero_init:
        in_specs.append(pl.BlockSpec((1, width - 1, c_dim), lambda b, t: (b, 0, 0)))
        args.append(hist)
    in_specs += [pl.BlockSpec((width, SUBLANES, c_dim), lambda b, t: (0, 0, 0)), vec, vec, vec]
    args += [w8, cb, lng, lnb]
    return pl.pallas_call(
        kern,
        grid=(nb, nt),
        in_specs=in_specs,
        out_specs=[
            pl.BlockSpec((tt, c_dim), lambda b, t: (rb0 + b * nt + t, 0)),
            pl.BlockSpec((1, width - 1, c_dim), lambda b, t: (b, 0, 0)),
        ],
        out_shape=[jax.ShapeDtypeStruct((n, c_dim), BF16),
                   jax.ShapeDtypeStruct((nb, width - 1, c_dim), F32)],
        scratch_shapes=[pltpu.VMEM((hp + tt, c_dim), F32), pltpu.VMEM((tt, c_dim), F32)],
        input_output_aliases={0: 0} if prev is not None else {},
        compiler_params=_params(("parallel", "arbitrary")),
        name="conv_branch",
    )(*args)


def _gla_chunk(q, k, v, lg, s, c):
    dk = q.shape[1]
    dv = v.shape[1]
    row = lax.broadcasted_iota(jnp.int32, (c, c), 0)
    col = lax.broadcasted_iota(jnp.int32, (c, c), 1)
    tri = jnp.where(row >= col, 1.0, 0.0).astype(BF16)
    lg3 = _split3(lg)
    b = _dot(tri, lg3[0]) + _dot(tri, lg3[1]) + _dot(tri, lg3[2])
    b_last = b[c - 1:c, :]
    vb = v.astype(BF16)

    o = _dot((q * jnp.exp(b)).astype(BF16), s.astype(BF16))
    kd = (k * jnp.exp(b_last - b)).astype(BF16)
    tdims = (((0,), (0,)), ((), ()))
    kv = lax.dot_general(kd, vb, tdims, preferred_element_type=F32)
    ones = jnp.ones((c, LANES), BF16)
    tot = (lax.dot_general(lg3[0], ones, tdims, preferred_element_type=F32)
           + lax.dot_general(lg3[1], ones, tdims, preferred_element_type=F32)
           + lax.dot_general(lg3[2], ones, tdims, preferred_element_type=F32))
    decay = jnp.exp(tot)
    s_new = s * jnp.concatenate([decay] * (dv // LANES), axis=1) + kv

    outs = []
    for i in range(c // SUB):
        r0 = i * SUB
        nk = r0 + SUB
        ref = b[r0 - 1:r0, :] if i > 0 else jnp.zeros((1, dk), F32)
        qi = (q[r0:nk] * jnp.exp(b[r0:nk] - ref)).astype(BF16)
        ki = (k[0:nk] * jnp.exp(ref - b[0:nk])).astype(BF16)
        a = lax.dot_general(qi, ki, (((1,), (1,)), ((), ())), preferred_element_type=F32)
        causal = (lax.broadcasted_iota(jnp.int32, (SUB, nk), 0) + r0
                  >= lax.broadcasted_iota(jnp.int32, (SUB, nk), 1))
        a = jnp.where(causal, a, 0.0)
        outs.append(_dot(a.astype(BF16), vb[0:nk]))
    o = o + (jnp.concatenate(outs, axis=0) if len(outs) > 1 else outs[0])
    return o, s_new


def _gla_kernel(*refs, t_len, scale, zero_init, has_prev):
    if has_prev:
        refs = refs[1:]
    if zero_init:
        q_ref, k_ref, v_ref, lg_ref, g_ref, ng_ref, o_ref, so_ref, s_ref = refs
    else:
        q_ref, k_ref, v_ref, lg_ref, g_ref, ng_ref, s0_ref, o_ref, so_ref, s_ref = refs

    if zero_init:
        s_ref[...] = jnp.zeros(s_ref.shape, F32)
    else:
        s_ref[...] = s0_ref[0, 0]

    def block(r0, c):
        rows = pl.ds(r0, c)
        o, s_new = _gla_chunk(q_ref[rows, :] * scale, k_ref[rows, :], v_ref[rows, :], lg_ref[rows, :],
                              s_ref[...], c)
        s_ref[...] = s_new
        on = _rms(o, ng_ref[...])
        g = g_ref[rows, :]
        o_ref[rows, :] = (on * (g * _sigmoid(g))).astype(BF16)

    first = t_len % CHUNK
    if first:
        block(0, first)

    def full(j, carry):
        block(pl.multiple_of(first + j * CHUNK, BF16_ROWS), CHUNK)
        return carry

    if t_len // CHUNK:
        lax.fori_loop(0, t_len // CHUNK, full, 0)
    so_ref[0, 0] = s_ref[...]


def _gla_branch(p, lg, ng, s0, *, n, nb, t_len, row0, col_q, col_k, col_v, col_g, n_heads, dkh, dvh,
                prev=None):
    assert row0 % t_len == 0 and t_len % BF16_ROWS == 0 and (t_len % CHUNK) % SUB == 0
    rb0 = row0 // t_len
    assert col_q % dkh == 0 and col_k % dkh == 0 and col_v % dvh == 0 and col_g % dvh == 0
    cq, ck, cv, cg = col_q // dkh, col_k // dkh, col_v // dvh, col_g // dvh
    zero_init = s0 is None
    kern = functools.partial(_gla_kernel, t_len=t_len, scale=float(dkh) ** -0.5, zero_init=zero_init,
                             has_prev=prev is not None)
    in_specs = [pl.BlockSpec(memory_space=pl.ANY)] if prev is not None else []
    args = [prev] if prev is not None else []
    in_specs += [
        pl.BlockSpec((t_len, dkh), lambda b, h: (rb0 + b, cq + h)),
        pl.BlockSpec((t_len, dkh), lambda b, h: (rb0 + b, ck + h)),
        pl.BlockSpec((t_len, dvh), lambda b, h: (rb0 + b, cv + h)),
        pl.BlockSpec((t_len, dkh), lambda b, h: (rb0 + b, h)),
        pl.BlockSpec((t_len, dvh), lambda b, h: (rb0 + b, cg + h)),
        pl.BlockSpec((1, dvh), lambda b, h: (0, 0)),
    ]
    args += [p, p, p, lg, p, ng]
    if not zero_init:
        in_specs.append(pl.BlockSpec((1, 1, dkh, dvh), lambda b, h: (b, h, 0, 0)))
        args.append(s0)
    return pl.pallas_call(
        kern,
        grid=(nb, n_heads),
        in_specs=in_specs,
        out_specs=[
            pl.BlockSpec((t_len, dvh), lambda b, h: (rb0 + b, h)),
            pl.BlockSpec((1, 1, dkh, dvh), lambda b, h: (b, h, 0, 0)),
        ],
        out_shape=[jax.ShapeDtypeStruct((n, n_heads * dvh), BF16),
                   jax.ShapeDtypeStruct((nb, n_heads, dkh, dvh), F32)],
        scratch_shapes=[pltpu.VMEM((dkh, dvh), F32)],
        input_output_aliases={0: 0} if prev is not None else {},
        compiler_params=_params(("parallel", "parallel")),
        name="gla_branch",
    )(*args)


def _mix_kernel(c_ref, og_ref, zc_ref, zg_ref, wc_ref, wg_ref, o_ref):
    a = _dot(c_ref[...], wc_ref[...])
    b = _dot(og_ref[...], wg_ref[...])
    o_ref[...] = (_sigmoid(zc_ref[...]) * a + _sigmoid(zg_ref[...]) * b).astype(BF16)


def _mix(c, og, p, wc, wg, col_zc, col_zg, tm, tn):
    n, d = c.shape
    d_out = wc.shape[1]
    assert col_zc % tn == 0 and col_zg % tn == 0
    jc, jg = col_zc // tn, col_zg // tn
    return pl.pallas_call(
        _mix_kernel,
        grid=(n // tm, d_out // tn),
        in_specs=[
            pl.BlockSpec((tm, d), lambda i, j: (i, 0)),
            pl.BlockSpec((tm, og.shape[1]), lambda i, j: (i, 0)),
            pl.BlockSpec((tm, tn), lambda i, j: (i, jc + j)),
            pl.BlockSpec((tm, tn), lambda i, j: (i, jg + j)),
            pl.BlockSpec((d, tn), lambda i, j: (0, j)),
            pl.BlockSpec((og.shape[1], tn), lambda i, j: (0, j)),
        ],
        out_specs=pl.BlockSpec((tm, tn), lambda i, j: (i, j)),
        out_shape=jax.ShapeDtypeStruct((n, d_out), BF16),
        compiler_params=_params(("parallel", "arbitrary")),
        name="mix",
    )(c, og, p, p, wc, wg)


def _resid_mm_kernel(a_ref, w_ref, x_ref, o_ref):
    o_ref[...] = x_ref[...] + _dot(a_ref[...], w_ref[...])


def _resid_mm(a, w, x, tm, tn):
    n, k = a.shape
    d_out = w.shape[1]
    return pl.pallas_call(
        _resid_mm_kernel,
        grid=(n // tm, d_out // tn),
        in_specs=[
            pl.BlockSpec((tm, k), lambda i, j: (i, 0)),
            pl.BlockSpec((k, tn), lambda i, j: (0, j)),
            pl.BlockSpec((tm, tn), lambda i, j: (i, j)),
        ],
        out_specs=pl.BlockSpec((tm, tn), lambda i, j: (i, j)),
        out_shape=jax.ShapeDtypeStruct((n, d_out), F32),
        compiler_params=_params(("parallel", "arbitrary")),
        name="out_proj",
    )(a, w, x)


def _ffn_kernel(x_ref, g_ref, w1_ref, w3_ref, w2_ref, o_ref, hb_ref):
    f = pl.program_id(1)

    @pl.when(f == 0)
    def _():
        x = x_ref[...]
        hb_ref[...] = _rms(x, g_ref[...]).astype(BF16)
        o_ref[...] = x

    h = hb_ref[...]
    a = _dot(h, w1_ref[...])
    b = _dot(h, w3_ref[...])
    t = (a * _sigmoid(a) * b).astype(BF16)
    o_ref[...] += _dot(t, w2_ref[...])


def _ffn(x, g, w1, w3, w2, tm, tf):
    n, d = x.shape
    dff = w1.shape[1]
    return pl.pallas_call(
        _ffn_kernel,
        grid=(n // tm, dff // tf),
        in_specs=[
            pl.BlockSpec((tm, d), lambda i, f: (i, 0)),
            pl.BlockSpec((1, d), lambda i, f: (0, 0)),
            pl.BlockSpec((d, tf), lambda i, f: (0, f)),
            pl.BlockSpec((d, tf), lambda i, f: (0, f)),
            pl.BlockSpec((tf, d), lambda i, f: (f, 0)),
        ],
        out_specs=pl.BlockSpec((tm, d), lambda i, f: (i, 0)),
        out_shape=jax.ShapeDtypeStruct((n, d), F32),
        scratch_shapes=[pltpu.VMEM((tm, d), BF16)],
        compiler_params=_params(("parallel", "arbitrary")),
        name="ffn",
    )(x, g, w1, w3, w2)


def _router_kernel(x_ref, g_ref, rw_ref, idx_ref, gate_ref, *, n_exp):
    h = _rms(x_ref[...], g_ref[...])
    logits = _dot_f32(h, rw_ref[...])
    lane = lax.broadcasted_iota(jnp.int32, logits.shape, 1).astype(F32)
    neg = jnp.float32(-jnp.inf)
    l1 = jnp.where(lane < n_exp, logits, neg)
    m1 = jnp.max(l1, axis=-1, keepdims=True)
    i1 = jnp.min(jnp.where(l1 == m1, lane, float(LANES)), axis=-1, keepdims=True)
    l2 = jnp.where(lane == i1, neg, l1)
    m2 = jnp.max(l2, axis=-1, keepdims=True)
    i2 = jnp.min(jnp.where(l2 == m2, lane, float(LANES)), axis=-1, keepdims=True)
    e = jnp.exp(m2 - m1)
    den = 1.0 + e
    idx_ref[...] = jnp.where(lane == 0, i1, jnp.where(lane == 1, i2, 0.0)).astype(jnp.int32)
    gate_ref[...] = jnp.where(lane == 0, 1.0 / den, jnp.where(lane == 1, e / den, 0.0))


def _router(x, g, rw, n_exp, tm):
    n, d = x.shape
    return pl.pallas_call(
        functools.partial(_router_kernel, n_exp=n_exp),
        grid=(n // tm,),
        in_specs=[
            pl.BlockSpec((tm, d), lambda i: (i, 0)),
            pl.BlockSpec((1, d), lambda i: (0, 0)),
            pl.BlockSpec((d, LANES), lambda i: (0, 0)),
        ],
        out_specs=[pl.BlockSpec((tm, LANES), lambda i: (i, 0)), pl.BlockSpec((tm, LANES), lambda i: (i, 0))],
        out_shape=[jax.ShapeDtypeStruct((n, LANES), jnp.int32), jax.ShapeDtypeStruct((n, LANES), F32)],
        compiler_params=_params(("parallel",)),
        name="router",
    )(x, g, rw)


def _moe_kernel(te_ref, nv_ref, tok_ref, dst_ref, x_hbm, g_ref, w1_ref, w3_ref, w2_ref, y_hbm,
                xg_ref, hb_ref, acc_ref, tok_s, dst_s, sem_g, sem_s, sem_i, *, tm):
    t = pl.program_id(0)
    f = pl.program_id(1)
    active = nv_ref[t] > 0

    def row_copy_in(r):
        return pltpu.make_async_copy(x_hbm.at[pl.ds(tok_s[0, r], 1)], xg_ref.at[pl.ds(r, 1)], sem_g)

    def row_copy_out(r):
        return pltpu.make_async_copy(acc_ref.at[pl.ds(r, 1)], y_hbm.at[pl.ds(dst_s[0, r], 1)], sem_s)

    @pl.when(active & (f == 0))
    def _():
        cp = pltpu.make_async_copy(tok_ref.at[0], tok_s, sem_i)
        cp.start()
        cp.wait()

        def issue(r, carry):
            row_copy_in(r).start()
            return carry

        lax.fori_loop(0, tm, issue, 0)

        def drain(r, carry):
            row_copy_in(r).wait()
            return carry

        lax.fori_loop(0, tm, drain, 0)
        hb_ref[...] = _rms(xg_ref[...], g_ref[...]).astype(BF16)
        acc_ref[...] = jnp.zeros(acc_ref.shape, F32)

    @pl.when(active)
    def _():
        h = hb_ref[...]
        a = _dot(h, w1_ref[0])
        b = _dot(h, w3_ref[0])
        u = (a * _sigmoid(a) * b).astype(BF16)
        acc_ref[...] += _dot(u, w2_ref[0])

    @pl.when(active & (f == pl.num_programs(1) - 1))
    def _():
        cp = pltpu.make_async_copy(dst_ref.at[0], dst_s, sem_i)
        cp.start()
        cp.wait()

        def issue(r, carry):
            row_copy_out(r).start()
            return carry

        lax.fori_loop(0, tm, issue, 0)

        def drain(r, carry):
            row_copy_out(r).wait()
            return carry

        lax.fori_loop(0, tm, drain, 0)


def _moe_experts(x, g, w1, w3, w2, tile_e, tile_nv, row_tok, row_dst, n_out_rows, tm, tf):
    n, d = x.shape
    dff = w1.shape[2]
    n_tiles = tile_e.shape[0]
    nf = dff // tf

    def wcol(t, f, te, nv):
        return (te[t], 0, jnp.where(nv[t] > 0, f, nf - 1))

    def wrow(t, f, te, nv):
        return (te[t], jnp.where(nv[t] > 0, f, nf - 1), 0)

    grid_spec = pltpu.PrefetchScalarGridSpec(
        num_scalar_prefetch=2,
        grid=(n_tiles, nf),
        in_specs=[
            pl.BlockSpec((1, 1, tm), lambda t, f, te, nv: (t, 0, 0)),
            pl.BlockSpec((1, 1, tm), lambda t, f, te, nv: (t, 0, 0)),
            pl.BlockSpec(memory_space=pl.ANY),
            pl.BlockSpec((1, d), lambda t, f, te, nv: (0, 0)),
            pl.BlockSpec((1, d, tf), wcol),
            pl.BlockSpec((1, d, tf), wcol),
            pl.BlockSpec((1, tf, d), wrow),
        ],
        out_specs=pl.BlockSpec(memory_space=pl.ANY),
        scratch_shapes=[
            pltpu.VMEM((tm, d), F32),
            pltpu.VMEM((tm, d), BF16),
            pltpu.VMEM((tm, d), F32),
            pltpu.SMEM((1, tm), jnp.int32),
            pltpu.SMEM((1, tm), jnp.int32),
            pltpu.SemaphoreType.DMA,
            pltpu.SemaphoreType.DMA,
            pltpu.SemaphoreType.DMA,
        ],
    )
    return pl.pallas_call(
        functools.partial(_moe_kernel, tm=tm),
        grid_spec=grid_spec,
        out_shape=jax.ShapeDtypeStruct((n_out_rows, d), F32),
        compiler_params=_params(("arbitrary", "arbitrary")),
        name="moe_experts",
    )(tile_e, tile_nv, row_tok, row_dst, x, g, w1, w3, w2)


def _moe_plan(eidx, n_exp, tm, n_pad):
    n = eidx.shape[0]
    rows = n * TOP_K
    flat_e = eidx.reshape(-1)
    order = jnp.argsort(flat_e, stable=True).astype(jnp.int32)
    sizes = jnp.bincount(flat_e, length=n_exp).astype(jnp.int32)
    starts = jnp.cumsum(sizes) - sizes
    tiles_per = (sizes + tm - 1) // tm
    tile_end = jnp.cumsum(tiles_per)
    tile_start = tile_end - tiles_per
    n_tiles = -(-rows // tm) + n_exp
    t = jnp.arange(n_tiles, dtype=jnp.int32)
    te = jnp.minimum(jnp.searchsorted(tile_end, t, side="right"), n_exp - 1).astype(jnp.int32)
    off = (t - tile_start[te]) * tm
    nv = jnp.where(t < tile_end[-1], jnp.clip(sizes[te] - off, 0, tm), 0).astype(jnp.int32)
    r = jnp.arange(tm, dtype=jnp.int32)[None, :]
    valid = r < nv[:, None]
    flat = order[jnp.clip(starts[te][:, None] + off[:, None] + r, 0, rows - 1)]
    tok = flat // TOP_K
    slot = flat - tok * TOP_K
    row_tok = jnp.where(valid, tok, 0)
    row_dst = jnp.where(valid, slot * n_pad + tok, n + r)
    return te, nv, row_tok[:, None, :], row_dst[:, None, :]


def _combine_kernel(x_ref, y0_ref, y1_ref, gate_ref, g_ref, o_ref, *, final):
    gate = gate_ref[...]
    x = x_ref[...] + (y0_ref[0] * gate[:, 0:1] + y1_ref[0] * gate[:, 1:2])
    o_ref[...] = _rms(x, g_ref[...]) if final else x


def _combine(x, y2, gates, g, tm, final):
    n, d = x.shape
    return pl.pallas_call(
        functools.partial(_combine_kernel, final=final),
        grid=(n // tm,),
        in_specs=[
            pl.BlockSpec((tm, d), lambda i: (i, 0)),
            pl.BlockSpec((1, tm, d), lambda i: (0, i, 0)),
            pl.BlockSpec((1, tm, d), lambda i: (1, i, 0)),
            pl.BlockSpec((tm, LANES), lambda i: (i, 0)),
            pl.BlockSpec((1, d), lambda i: (0, 0)),
        ],
        out_specs=pl.BlockSpec((tm, d), lambda i: (i, 0)),
        out_shape=jax.ShapeDtypeStruct((n, d), F32),
        compiler_params=_params(("parallel",)),
        name="combine",
    )(x, y2, y2, gates, g)


def _norm_kernel(x_ref, g_ref, o_ref):
    o_ref[...] = _rms(x_ref[...], g_ref[...])


def _norm(x, g, tm):
    n, d = x.shape
    return pl.pallas_call(
        _norm_kernel,
        grid=(n // tm,),
        in_specs=[pl.BlockSpec((tm, d), lambda i: (i, 0)), pl.BlockSpec((1, d), lambda i: (0, 0))],
        out_specs=pl.BlockSpec((tm, d), lambda i: (i, 0)),
        out_shape=jax.ShapeDtypeStruct((n, d), F32),
        compiler_params=_params(("parallel",)),
        name="final_norm",
    )(x, g)


def kernel(x_prompt, x_sample, cache_conv, state_gla, meta_tokens, norm_mix_g, w_in, w_gate_a2, b_gate_a,
           conv_w, conv_b, conv_ln_g, conv_ln_b, w_conv_out, gla_norm_g, w_gla_out, w_out, norm_ffn_g,
           ffn_w1, ffn_w3, ffn_w2, router_w, exp_w1, exp_w3, exp_w2, final_norm_g):
    bp, seq, d = x_prompt.shape
    bs, ts, _ = x_sample.shape
    n_meta = meta_tokens.shape[0]
    tp = seq + n_meta
    depth = w_in.shape[0]
    c_dim = conv_w.shape[2]
    width = conv_w.shape[1]
    n_heads, dkh, dvh = state_gla.shape[2:]
    dk, dv = n_heads * dkh, n_heads * dvh
    rank = w_gate_a2.shape[1]
    n_exp = router_w.shape[2]
    assert rank <= LANES and n_exp <= LANES

    meta = jnp.broadcast_to(meta_tokens[None].astype(x_prompt.dtype), (bp, n_meta, d))
    xp = jnp.concatenate([meta, x_prompt], axis=1).reshape(bp * tp, d)
    x = jnp.concatenate([xp, x_sample.reshape(bs * ts, d)], axis=0)
    n_p = bp * tp
    n = n_p + bs * ts

    col_ub = c_dim
    col_q = 2 * c_dim
    col_k = col_q + dk
    col_v = col_k + dk
    col_g = col_v + dv
    col_alr = col_g + dv
    col_zc = col_g + dv
    col_zg = col_zc + d
    n_main = col_zg + d

    tm_big = _pick_tile(n, 640, BF16_ROWS)
    tm_mid = _pick_tile(n, 640, BF16_ROWS)
    tn_in = _pick_tile(n_main, 1024, LANES)
    tn_d = _pick_tile(d, 512, LANES)
    tt_p = _pick_tile(tp, 704, BF16_ROWS)
    row = lambda v: v.reshape(1, -1).astype(F32)

    new_conv_p, new_conv_s, new_gla_p, new_gla_s = [], [], [], []
    for i in range(depth):
        wi = w_in[i]
        wm = jnp.concatenate([wi[:, :col_alr], wi[:, col_alr + rank:]], axis=1).astype(BF16)
        wlr = jnp.pad(wi[:, col_alr:col_alr + rank], ((0, 0), (0, LANES - rank))).astype(BF16)
        wa2 = jnp.pad(w_gate_a2[i], ((0, LANES - rank), (0, 0))).astype(F32)
        p, lg = _inproj(x, row(norm_mix_g[i]), wm, wlr, wa2, row(b_gate_a[i]), tm_big, tn_in)

        w8 = jnp.broadcast_to(conv_w[i][:, None, :], (width, SUBLANES, c_dim)).astype(F32)
        cargs = (w8, row(conv_b[i]), row(conv_ln_g[i]), row(conv_ln_b[i]))
        c_p, nc_p = _conv_branch(p, None, *cargs, n=n, nb=bp, t_len=tp, row0=0, tt=tt_p, c_dim=c_dim)
        c, nc_s = _conv_branch(p, cache_conv[i], *cargs, n=n, nb=bs, t_len=ts, row0=n_p, tt=ts, c_dim=c_dim,
                               prev=c_p)

        gargs = dict(n=n, col_q=col_q, col_k=col_k, col_v=col_v, col_g=col_g, n_heads=n_heads, dkh=dkh, dvh=dvh)
        ng = row(gla_norm_g[i])
        og_p, ns_p = _gla_branch(p, lg, ng, None, nb=bp, t_len=tp, row0=0, **gargs)
        og, ns_s = _gla_branch(p, lg, ng, state_gla[i], nb=bs, t_len=ts, row0=n_p, prev=og_p, **gargs)

        mixed = _mix(c, og, p, w_conv_out[i].astype(BF16), w_gla_out[i].astype(BF16), col_zc, col_zg,
                     tm_big, tn_d)
        x = _resid_mm(mixed, w_out[i].astype(BF16), x, tm_big, tn_d)

        j = i // 2
        if i % 2 == 0:
            tf = _pick_tile(ffn_w1.shape[2], 512, LANES)
            x = _ffn(x, row(norm_ffn_g[i]), ffn_w1[j].astype(BF16), ffn_w3[j].astype(BF16),
                     ffn_w2[j].astype(BF16), tm_mid, tf)
            if i == depth - 1:
                x = _norm(x, row(final_norm_g), tm_mid)
        else:
            rw = jnp.pad(router_w[j], ((0, 0), (0, LANES - n_exp))).astype(F32)
            eidx, gates = _router(x, row(norm_ffn_g[i]), rw, n_exp, tm_mid)
            tm_e = 1024 if n * TOP_K >= 8192 else 64
            n_pad = n + tm_e
            te, nv, row_tok, row_dst = _moe_plan(eidx[:, :TOP_K], n_exp, tm_e, n_pad)
            tf = _pick_tile(exp_w1.shape[3], 512, LANES)
            y2 = _moe_experts(x, row(norm_ffn_g[i]), exp_w1[j].astype(BF16), exp_w3[j].astype(BF16),
                              exp_w2[j].astype(BF16), te, nv, row_tok, row_dst, TOP_K * n_pad, tm_e, tf)
            y2 = y2.reshape(TOP_K, n_pad, d)
            x = _combine(x, y2, gates, row(final_norm_g), tm_mid, final=i == depth - 1)
        new_conv_p.append(nc_p)
        new_conv_s.append(nc_s)
        new_gla_p.append(ns_p)
        new_gla_s.append(ns_s)

    y_p = x[:n_p].reshape(bp, tp, d)[:, n_meta:]
    y_s = x[n_p:].reshape(bs, ts, d)
    return (y_p, y_s, jnp.stack(new_conv_p), jnp.stack(new_gla_p), jnp.stack(new_conv_s), jnp.stack(new_gla_s))
```

```python
import functools
import math

import jax
import jax.numpy as jnp
from jax import lax
from jax.experimental import pallas as pl
from jax.experimental.pallas import tpu as pltpu

CHUNK = 64
SUB = 16
GATE_TEMP = 16.0
EPS = 1e-6
TOP_K = 2
LANES = 128
SUBLANES = 8
BF16_ROWS = 16
VMEM_LIMIT_BYTES = 56 * 1024 * 1024

F32 = jnp.float32
BF16 = jnp.bfloat16


def _pick_tile(n, target, mult):
    best = None
    for d in range(mult, min(n, target) + 1, mult):
        if n % d == 0:
            best = d
    assert best is not None, (n, target, mult)
    return best


def _params(sem):
    return pltpu.CompilerParams(dimension_semantics=sem, vmem_limit_bytes=VMEM_LIMIT_BYTES)


def _dot(a, b):
    return jnp.dot(a, b, preferred_element_type=F32)


def _split3(x):
    hi = x.astype(BF16)
    r1 = x - hi.astype(F32)
    mid = r1.astype(BF16)
    lo = (r1 - mid.astype(F32)).astype(BF16)
    return hi, mid, lo


def _dot_f32(a, b):
    a_hi, a_mid, _ = _split3(a)
    b_hi, b_mid, _ = _split3(b)
    return _dot(a_hi, b_hi) + _dot(a_mid, b_hi) + _dot(a_hi, b_mid)


def _rms(x, g):
    return x * lax.rsqrt(jnp.mean(x * x, axis=-1, keepdims=True) + EPS) * g


def _sigmoid(x):
    return jax.nn.sigmoid(x)


def _log_sigmoid(z):
    return jnp.minimum(z, 0.0) - jnp.log1p(jnp.exp(-jnp.abs(z)))


def _inproj_kernel(x_ref, g_ref, wm_ref, wlr_ref, wa2_ref, ba_ref, p_ref, lg_ref, nb_ref):
    @pl.when(pl.program_id(1) == 0)
    def _():
        nb = _rms(x_ref[...], g_ref[...]).astype(BF16)
        nb_ref[...] = nb
        a_lr = _dot(nb, wlr_ref[...])
        z = _dot_f32(a_lr, wa2_ref[...]) + ba_ref[...]
        lg_ref[...] = _log_sigmoid(z) * (1.0 / GATE_TEMP)

    p_ref[...] = _dot(nb_ref[...], wm_ref[...])


def _inproj(x, g, wm, wlr, wa2, ba, tm, tn):
    n, d = x.shape
    n_out = wm.shape[1]
    dk = wa2.shape[1]
    return pl.pallas_call(
        _inproj_kernel,
        grid=(n // tm, n_out // tn),
        in_specs=[
            pl.BlockSpec((tm, d), lambda i, j: (i, 0)),
            pl.BlockSpec((1, d), lambda i, j: (0, 0)),
            pl.BlockSpec((d, tn), lambda i, j: (0, j)),
            pl.BlockSpec((d, LANES), lambda i, j: (0, 0)),
            pl.BlockSpec((LANES, dk), lambda i, j: (0, 0)),
            pl.BlockSpec((1, dk), lambda i, j: (0, 0)),
        ],
        out_specs=[
            pl.BlockSpec((tm, tn), lambda i, j: (i, j)),
            pl.BlockSpec((tm, dk), lambda i, j: (i, 0)),
        ],
        out_shape=[jax.ShapeDtypeStruct((n, n_out), F32), jax.ShapeDtypeStruct((n, dk), F32)],
        scratch_shapes=[pltpu.VMEM((tm, d), BF16)],
        compiler_params=_params(("parallel", "arbitrary")),
        name="inproj",
    )(x, g, wm, wlr, wa2, ba)


def _conv_kernel(*refs, tt, width, hp, lane_chunk, has_prev):
    if has_prev:
        refs = refs[1:]
    ua_ref, ub_ref, hist_ref, w_ref, cb_ref, lng_ref, lnb_ref, c_ref, nh_ref, buf_ref, y_ref = refs
    t = pl.program_id(1)
    hist = width - 1
    off = hp - hist
    d = ua_ref.shape[1]

    @pl.when(t == 0)
    def _():
        buf_ref[0:hp, :] = jnp.zeros((hp, d), F32)
        buf_ref[off:hp, :] = hist_ref[0]

    @pl.when(t > 0)
    def _():
        buf_ref[0:hp, :] = buf_ref[tt:tt + hp, :]

    buf_ref[hp:hp + tt, :] = ua_ref[...] * _sigmoid(ub_ref[...])

    n_groups = (off + width - 1) // SUBLANES + 2

    def rows(i, carry):
        r0 = pl.multiple_of(i * SUBLANES, SUBLANES)
        sub = lax.broadcasted_iota(jnp.int32, (SUBLANES, lane_chunk), 0)
        for lc in range(d // lane_chunk):
            cols = slice(lc * lane_chunk, (lc + 1) * lane_chunk)
            xs = [buf_ref[pl.ds(r0 + SUBLANES * g, SUBLANES), cols] for g in range(n_groups - 1)]
            total = None
            for s in range(SUBLANES):
                acc = None
                for w in range(width):
                    a, sw = divmod(off + w, SUBLANES)
                    if sw != s:
                        continue
                    x = xs[a] if s == 0 else jnp.where(sub >= s, xs[a], xs[a + 1])
                    term = x * w_ref[w, :, cols]
                    acc = term if acc is None else acc + term
                if acc is None:
                    continue
                if s:
                    acc = pltpu.roll(acc, SUBLANES - s, 0)
                total = acc if total is None else total + acc
            y_ref[pl.ds(r0, SUBLANES), cols] = total
        return carry

    lax.fori_loop(0, tt // SUBLANES, rows, 0)

    @pl.when(t == pl.num_programs(1) - 1)
    def _():
        nh_ref[0] = buf_ref[tt + off:tt + hp, :]

    y = y_ref[...] + cb_ref[...]
    mu = jnp.mean(y, axis=-1, keepdims=True)
    yc = y - mu
    c = yc * lax.rsqrt(jnp.mean(yc * yc, axis=-1, keepdims=True) + EPS) * lng_ref[...] + lnb_ref[...]
    c_ref[...] = (c * _sigmoid(c)).astype(BF16)


def _conv_branch(p, hist, w8, cb, lng, lnb, *, n, nb, t_len, row0, tt, c_dim, prev=None):
    width = w8.shape[0]
    hp = -(-(width - 1) // SUBLANES) * SUBLANES
    assert t_len % tt == 0 and row0 % tt == 0 and tt % BF16_ROWS == 0
    assert tt >= hp or t_len == tt
    nt = t_len // tt
    rb0 = row0 // tt
    kern = functools.partial(_conv_kernel, tt=tt, width=width, hp=hp,
                             lane_chunk=min(c_dim, 4 * LANES), has_prev=prev is not None)
    vec = pl.BlockSpec((1, c_dim), lambda b, t: (0, 0))
    in_specs = [pl.BlockSpec(memory_space=pl.ANY)] if prev is not None else []
    args = [prev] if prev is not None else []
    in_specs += [
        pl.BlockSpec((tt, c_dim), lambda b, t: (rb0 + b * nt + t, 0)),
        pl.BlockSpec((tt, c_dim), lambda b, t: (rb0 + b * nt + t, 1)),
        pl.BlockSpec((1, width - 1, c_dim), lambda b, t: (b, 0, 0)),
        pl.BlockSpec((width, SUBLANES, c_dim), lambda b, t: (0, 0, 0)), vec, vec, vec,
    ]
    args += [p, p, hist, w8, cb, lng, lnb]
    return pl.pallas_call(
        kern,
        grid=(nb, nt),
        in_specs=in_specs,
        out_specs=[
            pl.BlockSpec((tt, c_dim), lambda b, t: (rb0 + b * nt + t, 0)),
            pl.BlockSpec((1, width - 1, c_dim), lambda b, t: (b, 0, 0)),
        ],
        out_shape=[jax.ShapeDtypeStruct((n, c_dim), BF16),
                   jax.ShapeDtypeStruct((nb, width - 1, c_dim), F32)],
        scratch_shapes=[pltpu.VMEM((hp + tt, c_dim), F32), pltpu.VMEM((tt, c_dim), F32)],
        input_output_aliases={0: 0} if prev is not None else {},
        compiler_params=_params(("parallel", "arbitrary")),
        name="conv_branch",
    )(*args)


def _gla_block(q, k, v, lg, s_ref, c):
    n_heads, dkh, dvh = s_ref.shape
    dk = q.shape[1]
    row = lax.broadcasted_iota(jnp.int32, (c, c), 0)
    col = lax.broadcasted_iota(jnp.int32, (c, c), 1)
    tri = jnp.where(row >= col, 1.0, 0.0).astype(BF16)
    lg3 = _split3(lg)
    b = _dot(tri, lg3[0]) + _dot(tri, lg3[1]) + _dot(tri, lg3[2])
    b_last = b[c - 1:c, :]
    vb = v.astype(BF16)
    qe = (q * jnp.exp(b)).astype(BF16)
    kd = (k * jnp.exp(b_last - b)).astype(BF16)
    tdims = (((0,), (0,)), ((), ()))
    ones = jnp.ones((c, LANES), BF16)
    tot = (lax.dot_general(lg3[0], ones, tdims, preferred_element_type=F32)
           + lax.dot_general(lg3[1], ones, tdims, preferred_element_type=F32)
           + lax.dot_general(lg3[2], ones, tdims, preferred_element_type=F32))
    decay = jnp.exp(tot)

    qis, kis = [], []
    for i in range(c // SUB):
        r0 = i * SUB
        nk = r0 + SUB
        ref = b[r0 - 1:r0, :] if i > 0 else jnp.zeros((1, dk), F32)
        qis.append((q[r0:nk] * jnp.exp(b[r0:nk] - ref)).astype(BF16))
        kis.append((k[0:nk] * jnp.exp(ref - b[0:nk])).astype(BF16))

    kh = lambda x, h: x[:, h * dkh:(h + 1) * dkh]
    vh = lambda x, h: x[:, h * dvh:(h + 1) * dvh]
    heads = range(n_heads)
    o_state = [_dot(kh(qe, h), s_ref[h].astype(BF16)) for h in heads]
    kv = [lax.dot_general(kh(kd, h), vh(vb, h), tdims, preferred_element_type=F32) for h in heads]
    nt = (((1,), (1,)), ((), ()))
    a = [[lax.dot_general(kh(qis[i], h), kh(kis[i], h), nt, preferred_element_type=F32)
          for i in range(c // SUB)] for h in heads]
    outs = []
    for h in heads:
        parts = []
        for i in range(c // SUB):
            r0 = i * SUB
            nk = r0 + SUB
            causal = (lax.broadcasted_iota(jnp.int32, (SUB, nk), 0) + r0
                      >= lax.broadcasted_iota(jnp.int32, (SUB, nk), 1))
            parts.append(_dot(jnp.where(causal, a[h][i], 0.0).astype(BF16), vh(vb, h)[0:nk]))
        outs.append(o_state[h] + (jnp.concatenate(parts, axis=0) if len(parts) > 1 else parts[0]))
    for h in heads:
        dcol = decay[h * dkh:(h + 1) * dkh]
        s_ref[h] = s_ref[h] * jnp.concatenate([dcol] * (dvh // LANES), axis=1) + kv[h]
    return outs


def _gla_kernel(*refs, tt, c, n_heads, scale, has_prev):
    if has_prev:
        refs = refs[1:]
    q_ref, k_ref, v_ref, lg_ref, g_ref, ng_ref, s0_ref, o_ref, so_ref, s_ref = refs
    t = pl.program_id(1)
    dvh = s_ref.shape[2]

    @pl.when(t == 0)
    def _():
        s_ref[...] = s0_ref[0]

    def block(r0):
        rows = pl.ds(r0, c)
        outs = _gla_block(q_ref[rows, :] * scale, k_ref[rows, :], v_ref[rows, :], lg_ref[rows, :], s_ref, c)
        for h in range(n_heads):
            vc = slice(h * dvh, (h + 1) * dvh)
            g = g_ref[rows, vc]
            o_ref[rows, vc] = (_rms(outs[h], ng_ref[...]) * (g * _sigmoid(g))).astype(BF16)

    if tt == c:
        block(0)
    else:
        def body(j, carry):
            block(pl.multiple_of(j * c, c))
            return carry

        lax.fori_loop(0, tt // c, body, 0)

    @pl.when(t == pl.num_programs(1) - 1)
    def _():
        so_ref[0] = s_ref[...]


def _gla_branch(p, lg, ng, s0, *, n, nb, t_len, row0, tt, col_q, col_k, col_v, col_g, n_heads, dkh, dvh,
                prev=None):
    dk, dv = n_heads * dkh, n_heads * dvh
    c = min(CHUNK, t_len)
    assert t_len % tt == 0 and row0 % tt == 0 and tt % c == 0 and c % SUB == 0 and c % BF16_ROWS == 0
    nt = t_len // tt
    rb0 = row0 // tt
    assert col_q % dk == 0 and col_k % dk == 0 and col_v % dv == 0 and col_g % dv == 0
    cq, ck, cv, cg = col_q // dk, col_k // dk, col_v // dv, col_g // dv
    kern = functools.partial(_gla_kernel, tt=tt, c=c, n_heads=n_heads, scale=float(dkh) ** -0.5,
                             has_prev=prev is not None)
    in_specs = [pl.BlockSpec(memory_space=pl.ANY)] if prev is not None else []
    args = [prev] if prev is not None else []
    in_specs += [
        pl.BlockSpec((tt, dk), lambda b, t: (rb0 + b * nt + t, cq)),
        pl.BlockSpec((tt, dk), lambda b, t: (rb0 + b * nt + t, ck)),
        pl.BlockSpec((tt, dv), lambda b, t: (rb0 + b * nt + t, cv)),
        pl.BlockSpec((tt, dk), lambda b, t: (rb0 + b * nt + t, 0)),
        pl.BlockSpec((tt, dv), lambda b, t: (rb0 + b * nt + t, cg)),
        pl.BlockSpec((1, dvh), lambda b, t: (0, 0)),
        pl.BlockSpec((1, n_heads, dkh, dvh), lambda b, t: (b, 0, 0, 0)),
    ]
    args += [p, p, p, lg, p, ng, s0]
    return pl.pallas_call(
        kern,
        grid=(nb, nt),
        in_specs=in_specs,
        out_specs=[
            pl.BlockSpec((tt, dv), lambda b, t: (rb0 + b * nt + t, 0)),
            pl.BlockSpec((1, n_heads, dkh, dvh), lambda b, t: (b, 0, 0, 0)),
        ],
        out_shape=[jax.ShapeDtypeStruct((n, dv), BF16),
                   jax.ShapeDtypeStruct((nb, n_heads, dkh, dvh), F32)],
        scratch_shapes=[pltpu.VMEM((n_heads, dkh, dvh), F32)],
        input_output_aliases={0: 0} if prev is not None else {},
        compiler_params=_params(("parallel", "arbitrary")),
        name="gla_branch",
    )(*args)


def _mix_kernel(c_ref, og_ref, zc_ref, zg_ref, wc_ref, wg_ref, o_ref):
    a = _dot(c_ref[...], wc_ref[...])
    b = _dot(og_ref[...], wg_ref[...])
    o_ref[...] = (_sigmoid(zc_ref[...]) * a + _sigmoid(zg_ref[...]) * b).astype(BF16)


def _mix(c, og, p, wc, wg, col_zc, col_zg, tm, tn):
    n, d = c.shape
    d_out = wc.shape[1]
    assert col_zc % tn == 0 and col_zg % tn == 0
    jc, jg = col_zc // tn, col_zg // tn
    return pl.pallas_call(
        _mix_kernel,
        grid=(n // tm, d_out // tn),
        in_specs=[
            pl.BlockSpec((tm, d), lambda i, j: (i, 0)),
            pl.BlockSpec((tm, og.shape[1]), lambda i, j: (i, 0)),
            pl.BlockSpec((tm, tn), lambda i, j: (i, jc + j)),
            pl.BlockSpec((tm, tn), lambda i, j: (i, jg + j)),
            pl.BlockSpec((d, tn), lambda i, j: (0, j)),
            pl.BlockSpec((og.shape[1], tn), lambda i, j: (0, j)),
        ],
        out_specs=pl.BlockSpec((tm, tn), lambda i, j: (i, j)),
        out_shape=jax.ShapeDtypeStruct((n, d_out), BF16),
        compiler_params=_params(("parallel", "arbitrary")),
        name="mix",
    )(c, og, p, p, wc, wg)


def _resid_mm_kernel(a_ref, w_ref, x_ref, o_ref):
    o_ref[...] = x_ref[...] + _dot(a_ref[...], w_ref[...])


def _resid_mm(a, w, x, tm, tn):
    n, k = a.shape
    d_out = w.shape[1]
    return pl.pallas_call(
        _resid_mm_kernel,
        grid=(n // tm, d_out // tn),
        in_specs=[
            pl.BlockSpec((tm, k), lambda i, j: (i, 0)),
            pl.BlockSpec((k, tn), lambda i, j: (0, j)),
            pl.BlockSpec((tm, tn), lambda i, j: (i, j)),
        ],
        out_specs=pl.BlockSpec((tm, tn), lambda i, j: (i, j)),
        out_shape=jax.ShapeDtypeStruct((n, d_out), F32),
        compiler_params=_params(("parallel", "arbitrary")),
        name="out_proj",
    )(a, w, x)


def _ffn_kernel(x_ref, g_ref, w1_ref, w3_ref, w2_ref, o_ref, hb_ref):
    f = pl.program_id(1)

    @pl.when(f == 0)
    def _():
        x = x_ref[...]
        hb_ref[...] = _rms(x, g_ref[...]).astype(BF16)
        o_ref[...] = x

    h = hb_ref[...]
    a = _dot(h, w1_ref[...])
    b = _dot(h, w3_ref[...])
    t = (a * _sigmoid(a) * b).astype(BF16)
    o_ref[...] += _dot(t, w2_ref[...])


def _ffn(x, g, w1, w3, w2, tm, tf):
    n, d = x.shape
    dff = w1.shape[1]
    return pl.pallas_call(
        _ffn_kernel,
        grid=(n // tm, dff // tf),
        in_specs=[
            pl.BlockSpec((tm, d), lambda i, f: (i, 0)),
            pl.BlockSpec((1, d), lambda i, f: (0, 0)),
            pl.BlockSpec((d, tf), lambda i, f: (0, f)),
            pl.BlockSpec((d, tf), lambda i, f: (0, f)),
            pl.BlockSpec((tf, d), lambda i, f: (f, 0)),
        ],
        out_specs=pl.BlockSpec((tm, d), lambda i, f: (i, 0)),
        out_shape=jax.ShapeDtypeStruct((n, d), F32),
        scratch_shapes=[pltpu.VMEM((tm, d), BF16)],
        compiler_params=_params(("parallel", "arbitrary")),
        name="ffn",
    )(x, g, w1, w3, w2)


def _router_kernel(x_ref, g_ref, rw_ref, idx_ref, gate_ref, *, n_exp):
    h = _rms(x_ref[...], g_ref[...])
    logits = _dot_f32(h, rw_ref[...])
    lane = lax.broadcasted_iota(jnp.int32, logits.shape, 1).astype(F32)
    neg = jnp.float32(-jnp.inf)
    l1 = jnp.where(lane < n_exp, logits, neg)
    m1 = jnp.max(l1, axis=-1, keepdims=True)
    i1 = jnp.min(jnp.where(l1 == m1, lane, float(LANES)), axis=-1, keepdims=True)
    l2 = jnp.where(lane == i1, neg, l1)
    m2 = jnp.max(l2, axis=-1, keepdims=True)
    i2 = jnp.min(jnp.where(l2 == m2, lane, float(LANES)), axis=-1, keepdims=True)
    e = jnp.exp(m2 - m1)
    den = 1.0 + e
    idx_ref[...] = jnp.where(lane == 0, i1, jnp.where(lane == 1, i2, 0.0)).astype(jnp.int32)
    gate_ref[...] = jnp.where(lane == 0, 1.0 / den, jnp.where(lane == 1, e / den, 0.0))


def _router(x, g, rw, n_exp, tm):
    n, d = x.shape
    return pl.pallas_call(
        functools.partial(_router_kernel, n_exp=n_exp),
        grid=(n // tm,),
        in_specs=[
            pl.BlockSpec((tm, d), lambda i: (i, 0)),
            pl.BlockSpec((1, d), lambda i: (0, 0)),
            pl.BlockSpec((d, LANES), lambda i: (0, 0)),
        ],
        out_specs=[pl.BlockSpec((tm, LANES), lambda i: (i, 0)), pl.BlockSpec((tm, LANES), lambda i: (i, 0))],
        out_shape=[jax.ShapeDtypeStruct((n, LANES), jnp.int32), jax.ShapeDtypeStruct((n, LANES), F32)],
        compiler_params=_params(("parallel",)),
        name="router",
    )(x, g, rw)


def _moe_kernel(te_ref, nv_ref, tok_ref, tokn_ref, dst_ref, x_hbm, g_ref, w1_ref, w3_ref, w2_ref, y_hbm,
                xg_ref, hb_ref, acc_ref, tok_s, dst_s, sem_g, sem_s, sem_i, *, tm):
    t = pl.program_id(0)
    f = pl.program_id(1)
    n_tiles = pl.num_programs(0)
    last_f = pl.num_programs(1) - 1
    active = nv_ref[t] > 0
    next_active = jnp.logical_and(t + 1 < n_tiles, nv_ref[jnp.minimum(t + 1, n_tiles - 1)] > 0)

    def start_gather(idx_ref):
        cp = pltpu.make_async_copy(idx_ref.at[0], tok_s, sem_i)
        cp.start()
        cp.wait()

        def issue(r, carry):
            pltpu.make_async_copy(x_hbm.at[pl.ds(tok_s[0, r], 1)], xg_ref.at[pl.ds(r, 1)], sem_g).start()
            return carry

        lax.fori_loop(0, tm, issue, 0, unroll=8)

    def wait_gather():
        pltpu.make_async_copy(x_hbm.at[pl.ds(0, tm)], xg_ref, sem_g).wait()

    def start_scatter():
        cp = pltpu.make_async_copy(dst_ref.at[0], dst_s, sem_i)
        cp.start()
        cp.wait()

        def issue(r, carry):
            pltpu.make_async_copy(acc_ref.at[pl.ds(r, 1)], y_hbm.at[pl.ds(dst_s[0, r], 1)], sem_s).start()
            return carry

        lax.fori_loop(0, tm, issue, 0, unroll=8)

    def wait_scatter():
        pltpu.make_async_copy(acc_ref, y_hbm.at[pl.ds(0, tm)], sem_s).wait()

    @pl.when(active & (f == 0))
    def _():
        @pl.when(t == 0)
        def _():
            start_gather(tok_ref)

        wait_gather()
        hb_ref[...] = _rms(xg_ref[...], g_ref[...]).astype(BF16)

        @pl.when(next_active)
        def _():
            start_gather(tokn_ref)

        @pl.when(t > 0)
        def _():
            wait_scatter()

    @pl.when(active)
    def _():
        h = hb_ref[...]
        a = _dot(h, w1_ref[0])
        b = _dot(h, w3_ref[0])
        u = (a * _sigmoid(a) * b).astype(BF16)

        @pl.when(f == 0)
        def _():
            acc_ref[...] = _dot(u, w2_ref[0])

        @pl.when(f > 0)
        def _():
            acc_ref[...] += _dot(u, w2_ref[0])

    @pl.when(active & (f == last_f))
    def _():
        start_scatter()

        @pl.when(jnp.logical_not(next_active))
        def _():
            wait_scatter()


def _moe_experts(x, g, w1, w3, w2, tile_e, tile_nv, row_tok, row_dst, n_out_rows, tm, tf):
    n, d = x.shape
    dff = w1.shape[2]
    n_tiles = tile_e.shape[0]
    nf = dff // tf

    def wcol(t, f, te, nv):
        return (te[t], 0, jnp.where(nv[t] > 0, f, nf - 1))

    def wrow(t, f, te, nv):
        return (te[t], jnp.where(nv[t] > 0, f, nf - 1), 0)

    grid_spec = pltpu.PrefetchScalarGridSpec(
        num_scalar_prefetch=2,
        grid=(n_tiles, nf),
        in_specs=[
            pl.BlockSpec((1, 1, tm), lambda t, f, te, nv: (t, 0, 0)),
            pl.BlockSpec((1, 1, tm), lambda t, f, te, nv: (jnp.minimum(t + 1, n_tiles - 1), 0, 0)),
            pl.BlockSpec((1, 1, tm), lambda t, f, te, nv: (t, 0, 0)),
            pl.BlockSpec(memory_space=pl.ANY),
            pl.BlockSpec((1, d), lambda t, f, te, nv: (0, 0)),
            pl.BlockSpec((1, d, tf), wcol),
            pl.BlockSpec((1, d, tf), wcol),
            pl.BlockSpec((1, tf, d), wrow),
        ],
        out_specs=pl.BlockSpec(memory_space=pl.ANY),
        scratch_shapes=[
            pltpu.VMEM((tm, d), F32),
            pltpu.VMEM((tm, d), BF16),
            pltpu.VMEM((tm, d), F32),
            pltpu.SMEM((1, tm), jnp.int32),
            pltpu.SMEM((1, tm), jnp.int32),
            pltpu.SemaphoreType.DMA,
            pltpu.SemaphoreType.DMA,
            pltpu.SemaphoreType.DMA,
        ],
    )
    return pl.pallas_call(
        functools.partial(_moe_kernel, tm=tm),
        grid_spec=grid_spec,
        out_shape=jax.ShapeDtypeStruct((n_out_rows, d), F32),
        compiler_params=_params(("arbitrary", "arbitrary")),
        name="moe_experts",
    )(tile_e, tile_nv, row_tok, row_tok, row_dst, x, g, w1, w3, w2)


def _moe_plan(eidx, n_exp, tm, n_pad):
    n = eidx.shape[0]
    rows = n * TOP_K
    flat_e = eidx.reshape(-1)
    order = jnp.argsort(flat_e, stable=True).astype(jnp.int32)
    sizes = jnp.bincount(flat_e, length=n_exp).astype(jnp.int32)
    starts = jnp.cumsum(sizes) - sizes
    tiles_per = (sizes + tm - 1) // tm
    tile_end = jnp.cumsum(tiles_per)
    tile_start = tile_end - tiles_per
    n_tiles = -(-rows // tm) + n_exp
    t = jnp.arange(n_tiles, dtype=jnp.int32)
    te = jnp.minimum(jnp.searchsorted(tile_end, t, side="right"), n_exp - 1).astype(jnp.int32)
    off = (t - tile_start[te]) * tm
    nv = jnp.where(t < tile_end[-1], jnp.clip(sizes[te] - off, 0, tm), 0).astype(jnp.int32)
    r = jnp.arange(tm, dtype=jnp.int32)[None, :]
    valid = r < nv[:, None]
    flat = order[jnp.clip(starts[te][:, None] + off[:, None] + r, 0, rows - 1)]
    tok = flat // TOP_K
    slot = flat - tok * TOP_K
    row_tok = jnp.where(valid, tok, 0)
    row_dst = jnp.where(valid, slot * n_pad + tok, n + r)
    return te, nv, row_tok[:, None, :], row_dst[:, None, :]


def _combine_kernel(x_ref, y0_ref, y1_ref, gate_ref, g_ref, o_ref, *, final):
    gate = gate_ref[...]
    x = x_ref[...] + (y0_ref[0] * gate[:, 0:1] + y1_ref[0] * gate[:, 1:2])
    o_ref[...] = _rms(x, g_ref[...]) if final else x


def _combine(x, y2, gates, g, tm, final, row0, n_rows):
    d = x.shape[1]
    assert row0 % tm == 0 and n_rows % tm == 0
    i0 = row0 // tm
    return pl.pallas_call(
        functools.partial(_combine_kernel, final=final),
        grid=(n_rows // tm,),
        in_specs=[
            pl.BlockSpec((tm, d), lambda i: (i0 + i, 0)),
            pl.BlockSpec((1, tm, d), lambda i: (0, i0 + i, 0)),
            pl.BlockSpec((1, tm, d), lambda i: (1, i0 + i, 0)),
            pl.BlockSpec((tm, LANES), lambda i: (i0 + i, 0)),
            pl.BlockSpec((1, d), lambda i: (0, 0)),
        ],
        out_specs=pl.BlockSpec((tm, d), lambda i: (i, 0)),
        out_shape=jax.ShapeDtypeStruct((n_rows, d), F32),
        compiler_params=_params(("parallel",)),
        name="combine",
    )(x, y2, y2, gates, g)


def _norm_kernel(x_ref, g_ref, o_ref):
    o_ref[...] = _rms(x_ref[...], g_ref[...])


def _norm(x, g, tm, row0, n_rows):
    d = x.shape[1]
    assert row0 % tm == 0 and n_rows % tm == 0
    i0 = row0 // tm
    return pl.pallas_call(
        _norm_kernel,
        grid=(n_rows // tm,),
        in_specs=[pl.BlockSpec((tm, d), lambda i: (i0 + i, 0)), pl.BlockSpec((1, d), lambda i: (0, 0))],
        out_specs=pl.BlockSpec((tm, d), lambda i: (i, 0)),
        out_shape=jax.ShapeDtypeStruct((n_rows, d), F32),
        compiler_params=_params(("parallel",)),
        name="final_norm",
    )(x, g)


def kernel(x_prompt, x_sample, cache_conv, state_gla, meta_tokens, norm_mix_g, w_in, w_gate_a2, b_gate_a,
           conv_w, conv_b, conv_ln_g, conv_ln_b, w_conv_out, gla_norm_g, w_gla_out, w_out, norm_ffn_g,
           ffn_w1, ffn_w3, ffn_w2, router_w, exp_w1, exp_w3, exp_w2, final_norm_g):
    bp, seq, d = x_prompt.shape
    bs, ts, _ = x_sample.shape
    n_meta = meta_tokens.shape[0]
    depth = w_in.shape[0]
    c_dim = conv_w.shape[2]
    width = conv_w.shape[1]
    n_heads, dkh, dvh = state_gla.shape[2:]
    dk, dv = n_heads * dkh, n_heads * dvh
    rank = w_gate_a2.shape[1]
    n_exp = router_w.shape[2]
    assert rank <= LANES and n_exp <= LANES

    assert n_meta == ts, "meta and sample blocks share the short-sequence calls"
    n_pm, n_mt, n_s = bp * seq, bp * n_meta, bs * ts
    row_short, row_s = n_pm, n_pm + n_mt
    n = n_pm + n_mt + n_s
    meta = jnp.broadcast_to(meta_tokens[None].astype(x_prompt.dtype), (bp, n_meta, d))
    x = jnp.concatenate([x_prompt.reshape(n_pm, d), meta.reshape(n_mt, d), x_sample.reshape(n_s, d)], axis=0)

    col_q = 2 * c_dim
    col_k = col_q + dk
    col_v = col_k + dk
    col_g = col_v + dv
    col_alr = col_g + dv
    col_zc = col_g + dv
    col_zg = col_zc + d
    n_main = col_zg + d

    tm_big = _pick_tile(n, 640, BF16_ROWS)
    tm_mid = _pick_tile(n, 640, BF16_ROWS)
    tn_in = _pick_tile(n_main, 1024, LANES)
    tn_d = _pick_tile(d, 512, LANES)
    tt_conv = _pick_tile(seq, 512, BF16_ROWS)
    tt_gla = _pick_tile(seq, 256, CHUNK)
    tm_out_p = _pick_tile(n_pm, 640, BF16_ROWS)
    tm_out_s = _pick_tile(math.gcd(row_s, n_s), 640, BF16_ROWS)
    row = lambda v: v.reshape(1, -1).astype(F32)

    new_conv_p, new_conv_s, new_gla_p, new_gla_s = [], [], [], []
    for i in range(depth):
        wi = w_in[i]
        wm = jnp.concatenate([wi[:, :col_alr], wi[:, col_alr + rank:]], axis=1).astype(BF16)
        wlr = jnp.pad(wi[:, col_alr:col_alr + rank], ((0, 0), (0, LANES - rank))).astype(BF16)
        wa2 = jnp.pad(w_gate_a2[i], ((0, LANES - rank), (0, 0))).astype(F32)
        p, lg = _inproj(x, row(norm_mix_g[i]), wm, wlr, wa2, row(b_gate_a[i]), tm_big, tn_in)

        w8 = jnp.broadcast_to(conv_w[i][:, None, :], (width, SUBLANES, c_dim)).astype(F32)
        cargs = (w8, row(conv_b[i]), row(conv_ln_g[i]), row(conv_ln_b[i]))
        hist0 = jnp.concatenate([jnp.zeros((bp,) + cache_conv.shape[2:], F32), cache_conv[i].astype(F32)], axis=0)
        c_short, nh_short = _conv_branch(p, hist0, *cargs, n=n, nb=bp + bs, t_len=ts, row0=row_short, tt=ts,
                                         c_dim=c_dim)
        c, nc_p = _conv_branch(p, nh_short, *cargs, n=n, nb=bp, t_len=seq, row0=0, tt=tt_conv, c_dim=c_dim,
                               prev=c_short)
        nc_s = nh_short[bp:]

        gargs = dict(n=n, col_q=col_q, col_k=col_k, col_v=col_v, col_g=col_g, n_heads=n_heads, dkh=dkh, dvh=dvh)
        ng = row(gla_norm_g[i])
        s0 = jnp.concatenate([jnp.zeros((bp,) + state_gla.shape[2:], F32), state_gla[i].astype(F32)], axis=0)
        og_short, st_short = _gla_branch(p, lg, ng, s0, nb=bp + bs, t_len=ts, row0=row_short, tt=ts, **gargs)
        og, ns_p = _gla_branch(p, lg, ng, st_short, nb=bp, t_len=seq, row0=0, tt=tt_gla, prev=og_short, **gargs)
        ns_s = st_short[bp:]

        mixed = _mix(c, og, p, w_conv_out[i].astype(BF16), w_gla_out[i].astype(BF16), col_zc, col_zg,
                     tm_big, tn_d)
        x = _resid_mm(mixed, w_out[i].astype(BF16), x, tm_big, tn_d)

        j = i // 2
        last = i == depth - 1
        if i % 2 == 0:
            tf = _pick_tile(ffn_w1.shape[2], 512, LANES)
            x = _ffn(x, row(norm_ffn_g[i]), ffn_w1[j].astype(BF16), ffn_w3[j].astype(BF16),
                     ffn_w2[j].astype(BF16), tm_mid, tf)
            if last:
                y_p = _norm(x, row(final_norm_g), tm_out_p, 0, n_pm)
                y_s = _norm(x, row(final_norm_g), tm_out_s, row_s, n_s)
        else:
            rw = jnp.pad(router_w[j], ((0, 0), (0, LANES - n_exp))).astype(F32)
            eidx, gates = _router(x, row(norm_ffn_g[i]), rw, n_exp, tm_mid)
            tm_e = 1024 if n * TOP_K >= 8192 else 64
            n_pad = n + tm_e
            te, nv, row_tok, row_dst = _moe_plan(eidx[:, :TOP_K], n_exp, tm_e, n_pad)
            tf = _pick_tile(exp_w1.shape[3], 512, LANES)
            y2 = _moe_experts(x, row(norm_ffn_g[i]), exp_w1[j].astype(BF16), exp_w3[j].astype(BF16),
                              exp_w2[j].astype(BF16), te, nv, row_tok, row_dst, TOP_K * n_pad, tm_e, tf)
            y2 = y2.reshape(TOP_K, n_pad, d)
            if last:
                y_p = _combine(x, y2, gates, row(final_norm_g), tm_out_p, True, 0, n_pm)
                y_s = _combine(x, y2, gates, row(final_norm_g), tm_out_s, True, row_s, n_s)
            else:
                x = _combine(x, y2, gates, row(final_norm_g), tm_mid, False, 0, n)
        new_conv_p.append(nc_p)
        new_conv_s.append(nc_s)
        new_gla_p.append(ns_p)
        new_gla_s.append(ns_s)

    return (y_p.reshape(bp, seq, d), y_s.reshape(bs, ts, d), jnp.stack(new_conv_p), jnp.stack(new_gla_p),
            jnp.stack(new_conv_s), jnp.stack(new_gla_s))
```

```python
import functools
import math

import jax
import jax.numpy as jnp
from jax import lax
from jax.experimental import pallas as pl
from jax.experimental.pallas import tpu as pltpu

CHUNK = 64
SUB = 16
GATE_TEMP = 16.0
EPS = 1e-6
TOP_K = 2
LANES = 128
SUBLANES = 8
BF16_ROWS = 16
VMEM_LIMIT_BYTES = 56 * 1024 * 1024
ROW_DMA_PRIORITY = 1

F32 = jnp.float32
BF16 = jnp.bfloat16


def _pick_tile(n, target, mult):
    best = None
    for d in range(mult, min(n, target) + 1, mult):
        if n % d == 0:
            best = d
    assert best is not None, (n, target, mult)
    return best


def _params(sem):
    return pltpu.CompilerParams(dimension_semantics=sem, vmem_limit_bytes=VMEM_LIMIT_BYTES)


def _dot(a, b):
    return jnp.dot(a, b, preferred_element_type=F32)


def _split3(x):
    hi = x.astype(BF16)
    r1 = x - hi.astype(F32)
    mid = r1.astype(BF16)
    lo = (r1 - mid.astype(F32)).astype(BF16)
    return hi, mid, lo


def _dot_f32(a, b):
    a_hi, a_mid, _ = _split3(a)
    b_hi, b_mid, _ = _split3(b)
    return _dot(a_hi, b_hi) + _dot(a_mid, b_hi) + _dot(a_hi, b_mid)


def _rms(x, g):
    return x * lax.rsqrt(jnp.mean(x * x, axis=-1, keepdims=True) + EPS) * g


def _sigmoid(x):
    return jax.nn.sigmoid(x)


def _log_sigmoid(z):
    return jnp.minimum(z, 0.0) - jnp.log1p(jnp.exp(-jnp.abs(z)))


def _inproj_kernel(x_ref, g_ref, wm_ref, wlr_ref, wa2_ref, ba_ref, p_ref, lg_ref, nb_ref):
    @pl.when(pl.program_id(1) == 0)
    def _():
        nb = _rms(x_ref[...], g_ref[...]).astype(BF16)
        nb_ref[...] = nb
        a_lr = _dot(nb, wlr_ref[...])
        z = _dot_f32(a_lr, wa2_ref[...]) + ba_ref[...]
        lg_ref[...] = _log_sigmoid(z) * (1.0 / GATE_TEMP)

    p_ref[...] = _dot(nb_ref[...], wm_ref[...])


def _inproj(x, g, wm, wlr, wa2, ba, tm, tn):
    n, d = x.shape
    n_out = wm.shape[1]
    dk = wa2.shape[1]
    return pl.pallas_call(
        _inproj_kernel,
        grid=(n // tm, n_out // tn),
        in_specs=[
            pl.BlockSpec((tm, d), lambda i, j: (i, 0)),
            pl.BlockSpec((1, d), lambda i, j: (0, 0)),
            pl.BlockSpec((d, tn), lambda i, j: (0, j)),
            pl.BlockSpec((d, LANES), lambda i, j: (0, 0)),
            pl.BlockSpec((LANES, dk), lambda i, j: (0, 0)),
            pl.BlockSpec((1, dk), lambda i, j: (0, 0)),
        ],
        out_specs=[
            pl.BlockSpec((tm, tn), lambda i, j: (i, j)),
            pl.BlockSpec((tm, dk), lambda i, j: (i, 0)),
        ],
        out_shape=[jax.ShapeDtypeStruct((n, n_out), F32), jax.ShapeDtypeStruct((n, dk), F32)],
        scratch_shapes=[pltpu.VMEM((tm, d), BF16)],
        compiler_params=_params(("parallel", "arbitrary")),
        name="inproj",
    )(x, g, wm, wlr, wa2, ba)


def _conv_kernel(*refs, tt, width, hp, lane_chunk, has_prev):
    if has_prev:
        refs = refs[1:]
    ua_ref, ub_ref, hist_ref, w_ref, cb_ref, lng_ref, lnb_ref, c_ref, nh_ref, buf_ref, y_ref = refs
    t = pl.program_id(1)
    hist = width - 1
    off = hp - hist
    d = ua_ref.shape[1]

    @pl.when(t == 0)
    def _():
        buf_ref[0:hp, :] = jnp.zeros((hp, d), F32)
        buf_ref[off:hp, :] = hist_ref[0]

    @pl.when(t > 0)
    def _():
        buf_ref[0:hp, :] = buf_ref[tt:tt + hp, :]

    buf_ref[hp:hp + tt, :] = ua_ref[...] * _sigmoid(ub_ref[...])

    n_groups = (off + width - 1) // SUBLANES + 2

    def rows(i, carry):
        r0 = pl.multiple_of(i * SUBLANES, SUBLANES)
        sub = lax.broadcasted_iota(jnp.int32, (SUBLANES, lane_chunk), 0)
        for lc in range(d // lane_chunk):
            cols = slice(lc * lane_chunk, (lc + 1) * lane_chunk)
            xs = [buf_ref[pl.ds(r0 + SUBLANES * g, SUBLANES), cols] for g in range(n_groups - 1)]
            total = None
            for s in range(SUBLANES):
                acc = None
                for w in range(width):
                    a, sw = divmod(off + w, SUBLANES)
                    if sw != s:
                        continue
                    x = xs[a] if s == 0 else jnp.where(sub >= s, xs[a], xs[a + 1])
                    term = x * w_ref[w, :, cols]
                    acc = term if acc is None else acc + term
                if acc is None:
                    continue
                if s:
                    acc = pltpu.roll(acc, SUBLANES - s, 0)
                total = acc if total is None else total + acc
            y_ref[pl.ds(r0, SUBLANES), cols] = total
        return carry

    lax.fori_loop(0, tt // SUBLANES, rows, 0)

    @pl.when(t == pl.num_programs(1) - 1)
    def _():
        nh_ref[0] = buf_ref[tt + off:tt + hp, :]

    y = y_ref[...] + cb_ref[...]
    mu = jnp.mean(y, axis=-1, keepdims=True)
    yc = y - mu
    c = yc * lax.rsqrt(jnp.mean(yc * yc, axis=-1, keepdims=True) + EPS) * lng_ref[...] + lnb_ref[...]
    c_ref[...] = (c * _sigmoid(c)).astype(BF16)


def _conv_branch(p, hist, w8, cb, lng, lnb, *, n, nb, t_len, row0, tt, c_dim, prev=None):
    width = w8.shape[0]
    hp = -(-(width - 1) // SUBLANES) * SUBLANES
    assert t_len % tt == 0 and row0 % tt == 0 and tt % BF16_ROWS == 0
    assert tt >= hp or t_len == tt
    nt = t_len // tt
    rb0 = row0 // tt
    kern = functools.partial(_conv_kernel, tt=tt, width=width, hp=hp,
                             lane_chunk=min(c_dim, 4 * LANES), has_prev=prev is not None)
    vec = pl.BlockSpec((1, c_dim), lambda b, t: (0, 0))
    in_specs = [pl.BlockSpec(memory_space=pl.ANY)] if prev is not None else []
    args = [prev] if prev is not None else []
    in_specs += [
        pl.BlockSpec((tt, c_dim), lambda b, t: (rb0 + b * nt + t, 0)),
        pl.BlockSpec((tt, c_dim), lambda b, t: (rb0 + b * nt + t, 1)),
        pl.BlockSpec((1, width - 1, c_dim), lambda b, t: (b, 0, 0)),
        pl.BlockSpec((width, SUBLANES, c_dim), lambda b, t: (0, 0, 0)), vec, vec, vec,
    ]
    args += [p, p, hist, w8, cb, lng, lnb]
    return pl.pallas_call(
        kern,
        grid=(nb, nt),
        in_specs=in_specs,
        out_specs=[
            pl.BlockSpec((tt, c_dim), lambda b, t: (rb0 + b * nt + t, 0)),
            pl.BlockSpec((1, width - 1, c_dim), lambda b, t: (b, 0, 0)),
        ],
        out_shape=[jax.ShapeDtypeStruct((n, c_dim), BF16),
                   jax.ShapeDtypeStruct((nb, width - 1, c_dim), F32)],
        scratch_shapes=[pltpu.VMEM((hp + tt, c_dim), F32), pltpu.VMEM((tt, c_dim), F32)],
        input_output_aliases={0: 0} if prev is not None else {},
        compiler_params=_params(("parallel", "arbitrary")),
        name="conv_branch",
    )(*args)


def _gla_block(q, k, v, lg, s_ref, c):
    n_heads, dkh, dvh = s_ref.shape
    dk = q.shape[1]
    row = lax.broadcasted_iota(jnp.int32, (c, c), 0)
    col = lax.broadcasted_iota(jnp.int32, (c, c), 1)
    tri = jnp.where(row >= col, 1.0, 0.0).astype(BF16)
    lg3 = _split3(lg)
    b = _dot(tri, lg3[0]) + _dot(tri, lg3[1]) + _dot(tri, lg3[2])
    b_last = b[c - 1:c, :]
    vb = v.astype(BF16)
    qe = (q * jnp.exp(b)).astype(BF16)
    kd = (k * jnp.exp(b_last - b)).astype(BF16)
    tdims = (((0,), (0,)), ((), ()))
    ones = jnp.ones((c, LANES), BF16)
    tot = (lax.dot_general(lg3[0], ones, tdims, preferred_element_type=F32)
           + lax.dot_general(lg3[1], ones, tdims, preferred_element_type=F32)
           + lax.dot_general(lg3[2], ones, tdims, preferred_element_type=F32))
    decay = jnp.exp(tot)

    qis, kis = [], []
    for i in range(c // SUB):
        r0 = i * SUB
        nk = r0 + SUB
        ref = b[r0 - 1:r0, :] if i > 0 else jnp.zeros((1, dk), F32)
        qis.append((q[r0:nk] * jnp.exp(b[r0:nk] - ref)).astype(BF16))
        kis.append((k[0:nk] * jnp.exp(ref - b[0:nk])).astype(BF16))

    kh = lambda x, h: x[:, h * dkh:(h + 1) * dkh]
    vh = lambda x, h: x[:, h * dvh:(h + 1) * dvh]
    heads = range(n_heads)
    o_state = [_dot(kh(qe, h), s_ref[h].astype(BF16)) for h in heads]
    kv = [lax.dot_general(kh(kd, h), vh(vb, h), tdims, preferred_element_type=F32) for h in heads]
    nt = (((1,), (1,)), ((), ()))
    a = [[lax.dot_general(kh(qis[i], h), kh(kis[i], h), nt, preferred_element_type=F32)
          for i in range(c // SUB)] for h in heads]
    outs = []
    for h in heads:
        parts = []
        for i in range(c // SUB):
            r0 = i * SUB
            nk = r0 + SUB
            causal = (lax.broadcasted_iota(jnp.int32, (SUB, nk), 0) + r0
                      >= lax.broadcasted_iota(jnp.int32, (SUB, nk), 1))
            parts.append(_dot(jnp.where(causal, a[h][i], 0.0).astype(BF16), vh(vb, h)[0:nk]))
        outs.append(o_state[h] + (jnp.concatenate(parts, axis=0) if len(parts) > 1 else parts[0]))
    for h in heads:
        dcol = decay[h * dkh:(h + 1) * dkh]
        s_ref[h] = s_ref[h] * jnp.concatenate([dcol] * (dvh // LANES), axis=1) + kv[h]
    return outs


def _gla_kernel(*refs, tt, c, n_heads, scale, has_prev):
    if has_prev:
        refs = refs[1:]
    q_ref, k_ref, v_ref, lg_ref, g_ref, ng_ref, s0_ref, o_ref, so_ref, s_ref = refs
    t = pl.program_id(1)
    dvh = s_ref.shape[2]

    @pl.when(t == 0)
    def _():
        s_ref[...] = s0_ref[0]

    def block(r0):
        rows = pl.ds(r0, c)
        outs = _gla_block(q_ref[rows, :] * scale, k_ref[rows, :], v_ref[rows, :], lg_ref[rows, :], s_ref, c)
        for h in range(n_heads):
            vc = slice(h * dvh, (h + 1) * dvh)
            g = g_ref[rows, vc]
            o_ref[rows, vc] = (_rms(outs[h], ng_ref[...]) * (g * _sigmoid(g))).astype(BF16)

    if tt == c:
        block(0)
    else:
        def body(j, carry):
            block(pl.multiple_of(j * c, c))
            return carry

        lax.fori_loop(0, tt // c, body, 0)

    @pl.when(t == pl.num_programs(1) - 1)
    def _():
        so_ref[0] = s_ref[...]


def _gla_branch(p, lg, ng, s0, *, n, nb, t_len, row0, tt, col_q, col_k, col_v, col_g, n_heads, dkh, dvh,
                prev=None):
    dk, dv = n_heads * dkh, n_heads * dvh
    c = min(CHUNK, t_len)
    assert t_len % tt == 0 and row0 % tt == 0 and tt % c == 0 and c % SUB == 0 and c % BF16_ROWS == 0
    nt = t_len // tt
    rb0 = row0 // tt
    assert col_q % dk == 0 and col_k % dk == 0 and col_v % dv == 0 and col_g % dv == 0
    cq, ck, cv, cg = col_q // dk, col_k // dk, col_v // dv, col_g // dv
    kern = functools.partial(_gla_kernel, tt=tt, c=c, n_heads=n_heads, scale=float(dkh) ** -0.5,
                             has_prev=prev is not None)
    in_specs = [pl.BlockSpec(memory_space=pl.ANY)] if prev is not None else []
    args = [prev] if prev is not None else []
    in_specs += [
        pl.BlockSpec((tt, dk), lambda b, t: (rb0 + b * nt + t, cq)),
        pl.BlockSpec((tt, dk), lambda b, t: (rb0 + b * nt + t, ck)),
        pl.BlockSpec((tt, dv), lambda b, t: (rb0 + b * nt + t, cv)),
        pl.BlockSpec((tt, dk), lambda b, t: (rb0 + b * nt + t, 0)),
        pl.BlockSpec((tt, dv), lambda b, t: (rb0 + b * nt + t, cg)),
        pl.BlockSpec((1, dvh), lambda b, t: (0, 0)),
        pl.BlockSpec((1, n_heads, dkh, dvh), lambda b, t: (b, 0, 0, 0)),
    ]
    args += [p, p, p, lg, p, ng, s0]
    return pl.pallas_call(
        kern,
        grid=(nb, nt),
        in_specs=in_specs,
        out_specs=[
            pl.BlockSpec((tt, dv), lambda b, t: (rb0 + b * nt + t, 0)),
            pl.BlockSpec((1, n_heads, dkh, dvh), lambda b, t: (b, 0, 0, 0)),
        ],
        out_shape=[jax.ShapeDtypeStruct((n, dv), BF16),
                   jax.ShapeDtypeStruct((nb, n_heads, dkh, dvh), F32)],
        scratch_shapes=[pltpu.VMEM((n_heads, dkh, dvh), F32)],
        input_output_aliases={0: 0} if prev is not None else {},
        compiler_params=_params(("parallel", "arbitrary")),
        name="gla_branch",
    )(*args)


def _mix_kernel(c_ref, og_ref, zc_ref, zg_ref, wc_ref, wg_ref, o_ref):
    a = _dot(c_ref[...], wc_ref[...])
    b = _dot(og_ref[...], wg_ref[...])
    o_ref[...] = (_sigmoid(zc_ref[...]) * a + _sigmoid(zg_ref[...]) * b).astype(BF16)


def _mix(c, og, p, wc, wg, col_zc, col_zg, tm, tn):
    n, d = c.shape
    d_out = wc.shape[1]
    assert col_zc % tn == 0 and col_zg % tn == 0
    jc, jg = col_zc // tn, col_zg // tn
    return pl.pallas_call(
        _mix_kernel,
        grid=(n // tm, d_out // tn),
        in_specs=[
            pl.BlockSpec((tm, d), lambda i, j: (i, 0)),
            pl.BlockSpec((tm, og.shape[1]), lambda i, j: (i, 0)),
            pl.BlockSpec((tm, tn), lambda i, j: (i, jc + j)),
            pl.BlockSpec((tm, tn), lambda i, j: (i, jg + j)),
            pl.BlockSpec((d, tn), lambda i, j: (0, j)),
            pl.BlockSpec((og.shape[1], tn), lambda i, j: (0, j)),
        ],
        out_specs=pl.BlockSpec((tm, tn), lambda i, j: (i, j)),
        out_shape=jax.ShapeDtypeStruct((n, d_out), BF16),
        compiler_params=_params(("parallel", "arbitrary")),
        name="mix",
    )(c, og, p, p, wc, wg)


def _resid_mm_kernel(a_ref, w_ref, x_ref, o_ref):
    o_ref[...] = x_ref[...] + _dot(a_ref[...], w_ref[...])


def _resid_mm(a, w, x, tm, tn):
    n, k = a.shape
    d_out = w.shape[1]
    return pl.pallas_call(
        _resid_mm_kernel,
        grid=(n // tm, d_out // tn),
        in_specs=[
            pl.BlockSpec((tm, k), lambda i, j: (i, 0)),
            pl.BlockSpec((k, tn), lambda i, j: (0, j)),
            pl.BlockSpec((tm, tn), lambda i, j: (i, j)),
        ],
        out_specs=pl.BlockSpec((tm, tn), lambda i, j: (i, j)),
        out_shape=jax.ShapeDtypeStruct((n, d_out), F32),
        compiler_params=_params(("parallel", "arbitrary")),
        name="out_proj",
    )(a, w, x)


def _ffn_kernel(x_ref, g_ref, w1_ref, w3_ref, w2_ref, o_ref, hb_ref):
    f = pl.program_id(1)

    @pl.when(f == 0)
    def _():
        x = x_ref[...]
        hb_ref[...] = _rms(x, g_ref[...]).astype(BF16)
        o_ref[...] = x

    h = hb_ref[...]
    a = _dot(h, w1_ref[...])
    b = _dot(h, w3_ref[...])
    t = (a * _sigmoid(a) * b).astype(BF16)
    o_ref[...] += _dot(t, w2_ref[...])


def _ffn(x, g, w1, w3, w2, tm, tf):
    n, d = x.shape
    dff = w1.shape[1]
    return pl.pallas_call(
        _ffn_kernel,
        grid=(n // tm, dff // tf),
        in_specs=[
            pl.BlockSpec((tm, d), lambda i, f: (i, 0), pipeline_mode=pl.Buffered(1)),
            pl.BlockSpec((1, d), lambda i, f: (0, 0)),
            pl.BlockSpec((d, tf), lambda i, f: (0, f)),
            pl.BlockSpec((d, tf), lambda i, f: (0, f)),
            pl.BlockSpec((tf, d), lambda i, f: (f, 0)),
        ],
        out_specs=pl.BlockSpec((tm, d), lambda i, f: (i, 0)),
        out_shape=jax.ShapeDtypeStruct((n, d), F32),
        scratch_shapes=[pltpu.VMEM((tm, d), BF16)],
        compiler_params=_params(("parallel", "arbitrary")),
        name="ffn",
    )(x, g, w1, w3, w2)


def _router_kernel(x_ref, g_ref, rw_ref, idx_ref, gate_ref, *, n_exp):
    h = _rms(x_ref[...], g_ref[...])
    logits = _dot_f32(h, rw_ref[...])
    lane = lax.broadcasted_iota(jnp.int32, logits.shape, 1).astype(F32)
    neg = jnp.float32(-jnp.inf)
    l1 = jnp.where(lane < n_exp, logits, neg)
    m1 = jnp.max(l1, axis=-1, keepdims=True)
    i1 = jnp.min(jnp.where(l1 == m1, lane, float(LANES)), axis=-1, keepdims=True)
    l2 = jnp.where(lane == i1, neg, l1)
    m2 = jnp.max(l2, axis=-1, keepdims=True)
    i2 = jnp.min(jnp.where(l2 == m2, lane, float(LANES)), axis=-1, keepdims=True)
    e = jnp.exp(m2 - m1)
    den = 1.0 + e
    idx_ref[...] = jnp.where(lane == 0, i1, jnp.where(lane == 1, i2, 0.0)).astype(jnp.int32)
    gate_ref[...] = jnp.where(lane == 0, 1.0 / den, jnp.where(lane == 1, e / den, 0.0))


def _router(x, g, rw, n_exp, tm):
    n, d = x.shape
    return pl.pallas_call(
        functools.partial(_router_kernel, n_exp=n_exp),
        grid=(n // tm,),
        in_specs=[
            pl.BlockSpec((tm, d), lambda i: (i, 0)),
            pl.BlockSpec((1, d), lambda i: (0, 0)),
            pl.BlockSpec((d, LANES), lambda i: (0, 0)),
        ],
        out_specs=[pl.BlockSpec((tm, LANES), lambda i: (i, 0)), pl.BlockSpec((tm, LANES), lambda i: (i, 0))],
        out_shape=[jax.ShapeDtypeStruct((n, LANES), jnp.int32), jax.ShapeDtypeStruct((n, LANES), F32)],
        compiler_params=_params(("parallel",)),
        name="router",
    )(x, g, rw)


def _moe_kernel(te_ref, nv_ref, tok_ref, tokn_ref, dst_ref, x_hbm, g_ref, w1_ref, w3_ref, w2_ref, y_hbm,
                xg_ref, hb_ref, acc_ref, tok_s, dst_s, sem_g, sem_s, sem_i, *, tm):
    t = pl.program_id(0)
    f = pl.program_id(1)
    n_tiles = pl.num_programs(0)
    last_f = pl.num_programs(1) - 1
    active = nv_ref[t] > 0
    next_active = jnp.logical_and(t + 1 < n_tiles, nv_ref[jnp.minimum(t + 1, n_tiles - 1)] > 0)

    def start_gather(idx_ref):
        cp = pltpu.make_async_copy(idx_ref.at[0], tok_s, sem_i)
        cp.start()
        cp.wait()

        def issue(r, carry):
            pltpu.make_async_copy(x_hbm.at[pl.ds(tok_s[0, r], 1)], xg_ref.at[pl.ds(r, 1)], sem_g).start(
                priority=ROW_DMA_PRIORITY)
            return carry

        lax.fori_loop(0, tm, issue, 0, unroll=16)

    def wait_gather():
        pltpu.make_async_copy(x_hbm.at[pl.ds(0, tm)], xg_ref, sem_g).wait()

    def start_scatter():
        cp = pltpu.make_async_copy(dst_ref.at[0], dst_s, sem_i)
        cp.start()
        cp.wait()

        def issue(r, carry):
            pltpu.make_async_copy(acc_ref.at[pl.ds(r, 1)], y_hbm.at[pl.ds(dst_s[0, r], 1)], sem_s).start(
                priority=ROW_DMA_PRIORITY)
            return carry

        lax.fori_loop(0, tm, issue, 0, unroll=16)

    def wait_scatter():
        pltpu.make_async_copy(acc_ref, y_hbm.at[pl.ds(0, tm)], sem_s).wait()

    @pl.when(active & (f == 0))
    def _():
        @pl.when(t == 0)
        def _():
            start_gather(tok_ref)

        wait_gather()
        hb_ref[...] = _rms(xg_ref[...], g_ref[...]).astype(BF16)

        @pl.when(next_active)
        def _():
            start_gather(tokn_ref)

        @pl.when(t > 0)
        def _():
            wait_scatter()

    @pl.when(active)
    def _():
        h = hb_ref[...]
        a = _dot(h, w1_ref[0])
        b = _dot(h, w3_ref[0])
        u = (a * _sigmoid(a) * b).astype(BF16)

        @pl.when(f == 0)
        def _():
            acc_ref[...] = _dot(u, w2_ref[0])

        @pl.when(f > 0)
        def _():
            acc_ref[...] += _dot(u, w2_ref[0])

    @pl.when(active & (f == last_f))
    def _():
        start_scatter()

        @pl.when(jnp.logical_not(next_active))
        def _():
            wait_scatter()


def _moe_experts(x, g, w1, w3, w2, tile_e, tile_nv, row_tok, row_dst, n_out_rows, tm, tf):
    n, d = x.shape
    dff = w1.shape[2]
    n_tiles = tile_e.shape[0]
    nf = dff // tf

    def wcol(t, f, te, nv):
        return (te[t], 0, jnp.where(nv[t] > 0, f, nf - 1))

    def wrow(t, f, te, nv):
        return (te[t], jnp.where(nv[t] > 0, f, nf - 1), 0)

    grid_spec = pltpu.PrefetchScalarGridSpec(
        num_scalar_prefetch=2,
        grid=(n_tiles, nf),
        in_specs=[
            pl.BlockSpec((1, 1, tm), lambda t, f, te, nv: (t, 0, 0)),
            pl.BlockSpec((1, 1, tm), lambda t, f, te, nv: (jnp.minimum(t + 1, n_tiles - 1), 0, 0)),
            pl.BlockSpec((1, 1, tm), lambda t, f, te, nv: (t, 0, 0)),
            pl.BlockSpec(memory_space=pl.ANY),
            pl.BlockSpec((1, d), lambda t, f, te, nv: (0, 0)),
            pl.BlockSpec((1, d, tf), wcol),
            pl.BlockSpec((1, d, tf), wcol),
            pl.BlockSpec((1, tf, d), wrow),
        ],
        out_specs=pl.BlockSpec(memory_space=pl.ANY),
        scratch_shapes=[
            pltpu.VMEM((tm, d), F32),
            pltpu.VMEM((tm, d), BF16),
            pltpu.VMEM((tm, d), F32),
            pltpu.SMEM((1, tm), jnp.int32),
            pltpu.SMEM((1, tm), jnp.int32),
            pltpu.SemaphoreType.DMA,
            pltpu.SemaphoreType.DMA,
            pltpu.SemaphoreType.DMA,
        ],
    )
    return pl.pallas_call(
        functools.partial(_moe_kernel, tm=tm),
        grid_spec=grid_spec,
        out_shape=jax.ShapeDtypeStruct((n_out_rows, d), F32),
        compiler_params=_params(("arbitrary", "arbitrary")),
        name="moe_experts",
    )(tile_e, tile_nv, row_tok, row_tok, row_dst, x, g, w1, w3, w2)


def _moe_plan(eidx, n_exp, tm, n_pad):
    n = eidx.shape[0]
    rows = n * TOP_K
    flat_e = eidx.reshape(-1)
    order = jnp.argsort(flat_e, stable=True).astype(jnp.int32)
    sizes = jnp.bincount(flat_e, length=n_exp).astype(jnp.int32)
    starts = jnp.cumsum(sizes) - sizes
    tiles_per = (sizes + tm - 1) // tm
    tile_end = jnp.cumsum(tiles_per)
    tile_start = tile_end - tiles_per
    n_tiles = -(-rows // tm) + n_exp
    t = jnp.arange(n_tiles, dtype=jnp.int32)
    te = jnp.minimum(jnp.searchsorted(tile_end, t, side="right"), n_exp - 1).astype(jnp.int32)
    off = (t - tile_start[te]) * tm
    nv = jnp.where(t < tile_end[-1], jnp.clip(sizes[te] - off, 0, tm), 0).astype(jnp.int32)
    r = jnp.arange(tm, dtype=jnp.int32)[None, :]
    valid = r < nv[:, None]
    flat = order[jnp.clip(starts[te][:, None] + off[:, None] + r, 0, rows - 1)]
    tok = flat // TOP_K
    slot = flat - tok * TOP_K
    row_tok = jnp.where(valid, tok, 0)
    row_dst = jnp.where(valid, slot * n_pad + tok, n + r)
    return te, nv, row_tok[:, None, :], row_dst[:, None, :]


def _combine_kernel(x_ref, y0_ref, y1_ref, gate_ref, g_ref, o_ref, *, final):
    gate = gate_ref[...]
    x = x_ref[...] + (y0_ref[0] * gate[:, 0:1] + y1_ref[0] * gate[:, 1:2])
    o_ref[...] = _rms(x, g_ref[...]) if final else x


def _combine(x, y2, gates, g, tm, final, row0, n_rows):
    d = x.shape[1]
    assert row0 % tm == 0 and n_rows % tm == 0
    i0 = row0 // tm
    return pl.pallas_call(
        functools.partial(_combine_kernel, final=final),
        grid=(n_rows // tm,),
        in_specs=[
            pl.BlockSpec((tm, d), lambda i: (i0 + i, 0)),
            pl.BlockSpec((1, tm, d), lambda i: (0, i0 + i, 0)),
            pl.BlockSpec((1, tm, d), lambda i: (1, i0 + i, 0)),
            pl.BlockSpec((tm, LANES), lambda i: (i0 + i, 0)),
            pl.BlockSpec((1, d), lambda i: (0, 0)),
        ],
        out_specs=pl.BlockSpec((tm, d), lambda i: (i, 0)),
        out_shape=jax.ShapeDtypeStruct((n_rows, d), F32),
        compiler_params=_params(("parallel",)),
        name="combine",
    )(x, y2, y2, gates, g)


def _norm_kernel(x_ref, g_ref, o_ref):
    o_ref[...] = _rms(x_ref[...], g_ref[...])


def _norm(x, g, tm, row0, n_rows):
    d = x.shape[1]
    assert row0 % tm == 0 and n_rows % tm == 0
    i0 = row0 // tm
    return pl.pallas_call(
        _norm_kernel,
        grid=(n_rows // tm,),
        in_specs=[pl.BlockSpec((tm, d), lambda i: (i0 + i, 0)), pl.BlockSpec((1, d), lambda i: (0, 0))],
        out_specs=pl.BlockSpec((tm, d), lambda i: (i, 0)),
        out_shape=jax.ShapeDtypeStruct((n_rows, d), F32),
        compiler_params=_params(("parallel",)),
        name="final_norm",
    )(x, g)


def kernel(x_prompt, x_sample, cache_conv, state_gla, meta_tokens, norm_mix_g, w_in, w_gate_a2, b_gate_a,
           conv_w, conv_b, conv_ln_g, conv_ln_b, w_conv_out, gla_norm_g, w_gla_out, w_out, norm_ffn_g,
           ffn_w1, ffn_w3, ffn_w2, router_w, exp_w1, exp_w3, exp_w2, final_norm_g):
    bp, seq, d = x_prompt.shape
    bs, ts, _ = x_sample.shape
    n_meta = meta_tokens.shape[0]
    depth = w_in.shape[0]
    c_dim = conv_w.shape[2]
    width = conv_w.shape[1]
    n_heads, dkh, dvh = state_gla.shape[2:]
    dk, dv = n_heads * dkh, n_heads * dvh
    rank = w_gate_a2.shape[1]
    n_exp = router_w.shape[2]
    assert rank <= LANES and n_exp <= LANES

    assert n_meta == ts, "meta and sample blocks share the short-sequence calls"
    n_pm, n_mt, n_s = bp * seq, bp * n_meta, bs * ts
    row_short, row_s = n_pm, n_pm + n_mt
    n = n_pm + n_mt + n_s
    meta = jnp.broadcast_to(meta_tokens[None].astype(x_prompt.dtype), (bp, n_meta, d))
    x = jnp.concatenate([x_prompt.reshape(n_pm, d), meta.reshape(n_mt, d), x_sample.reshape(n_s, d)], axis=0)

    col_q = 2 * c_dim
    col_k = col_q + dk
    col_v = col_k + dk
    col_g = col_v + dv
    col_alr = col_g + dv
    col_zc = col_g + dv
    col_zg = col_zc + d
    n_main = col_zg + d

    tm_big = _pick_tile(n, 1280, BF16_ROWS)
    tm_mid = _pick_tile(n, 640, BF16_ROWS)
    tn_in = _pick_tile(n_main, 512, LANES)
    tn_d = _pick_tile(d, 512, LANES)
    tt_conv = _pick_tile(seq, 512, BF16_ROWS)
    tt_gla = _pick_tile(seq, 256, CHUNK)
    tm_out_p = _pick_tile(n_pm, 640, BF16_ROWS)
    tm_out_s = _pick_tile(math.gcd(row_s, n_s), 640, BF16_ROWS)
    row = lambda v: v.reshape(1, -1).astype(F32)

    new_conv_p, new_conv_s, new_gla_p, new_gla_s = [], [], [], []
    for i in range(depth):
        wi = w_in[i]
        wm = jnp.concatenate([wi[:, :col_alr], wi[:, col_alr + rank:]], axis=1).astype(BF16)
        wlr = jnp.pad(wi[:, col_alr:col_alr + rank], ((0, 0), (0, LANES - rank))).astype(BF16)
        wa2 = jnp.pad(w_gate_a2[i], ((0, LANES - rank), (0, 0))).astype(F32)
        p, lg = _inproj(x, row(norm_mix_g[i]), wm, wlr, wa2, row(b_gate_a[i]), tm_big, tn_in)

        w8 = jnp.broadcast_to(conv_w[i][:, None, :], (width, SUBLANES, c_dim)).astype(F32)
        cargs = (w8, row(conv_b[i]), row(conv_ln_g[i]), row(conv_ln_b[i]))
        hist0 = jnp.concatenate([jnp.zeros((bp,) + cache_conv.shape[2:], F32), cache_conv[i].astype(F32)], axis=0)
        c_short, nh_short = _conv_branch(p, hist0, *cargs, n=n, nb=bp + bs, t_len=ts, row0=row_short, tt=ts,
                                         c_dim=c_dim)
        c, nc_p = _conv_branch(p, nh_short, *cargs, n=n, nb=bp, t_len=seq, row0=0, tt=tt_conv, c_dim=c_dim,
                               prev=c_short)
        nc_s = nh_short[bp:]

        gargs = dict(n=n, col_q=col_q, col_k=col_k, col_v=col_v, col_g=col_g, n_heads=n_heads, dkh=dkh, dvh=dvh)
        ng = row(gla_norm_g[i])
        s0 = jnp.concatenate([jnp.zeros((bp,) + state_gla.shape[2:], F32), state_gla[i].astype(F32)], axis=0)
        og_short, st_short = _gla_branch(p, lg, ng, s0, nb=bp + bs, t_len=ts, row0=row_short, tt=ts, **gargs)
        og, ns_p = _gla_branch(p, lg, ng, st_short, nb=bp, t_len=seq, row0=0, tt=tt_gla, prev=og_short, **gargs)
        ns_s = st_short[bp:]

        mixed = _mix(c, og, p, w_conv_out[i].astype(BF16), w_gla_out[i].astype(BF16), col_zc, col_zg,
                     tm_big, tn_d)
        x = _resid_mm(mixed, w_out[i].astype(BF16), x, tm_mid, d)

        j = i // 2
        last = i == depth - 1
        if i % 2 == 0:
            tf = _pick_tile(ffn_w1.shape[2], 512, LANES)
            x = _ffn(x, row(norm_ffn_g[i]), ffn_w1[j].astype(BF16), ffn_w3[j].astype(BF16),
                     ffn_w2[j].astype(BF16), tm_big, tf)
            if last:
                y_p = _norm(x, row(final_norm_g), tm_out_p, 0, n_pm)
                y_s = _norm(x, row(final_norm_g), tm_out_s, row_s, n_s)
        else:
            rw = jnp.pad(router_w[j], ((0, 0), (0, LANES - n_exp))).astype(F32)
            eidx, gates = _router(x, row(norm_ffn_g[i]), rw, n_exp, tm_mid)
            tm_e = 1024 if n * TOP_K >= 8192 else 64
            n_pad = n + tm_e
            te, nv, row_tok, row_dst = _moe_plan(eidx[:, :TOP_K], n_exp, tm_e, n_pad)
            tf = _pick_tile(exp_w1.shape[3], 512, LANES)
            y2 = _moe_experts(x, row(norm_ffn_g[i]), exp_w1[j].astype(BF16), exp_w3[j].astype(BF16),
                              exp_w2[j].astype(BF16), te, nv, row_tok, row_dst, TOP_K * n_pad, tm_e, tf)
            y2 = y2.reshape(TOP_K, n_pad, d)
            if last:
                y_p = _combine(x, y2, gates, row(final_norm_g), tm_out_p, True, 0, n_pm)
                y_s = _combine(x, y2, gates, row(final_norm_g), tm_out_s, True, row_s, n_s)
            else:
                x = _combine(x, y2, gates, row(final_norm_g), tm_mid, False, 0, n)
        new_conv_p.append(nc_p)
        new_conv_s.append(nc_s)
        new_gla_p.append(ns_p)
        new_gla_s.append(ns_s)

    return (y_p.reshape(bp, seq, d), y_s.reshape(bs, ts, d), jnp.stack(new_conv_p), jnp.stack(new_gla_p),
            jnp.stack(new_conv_s), jnp.stack(new_gla_s))
```

```python
import functools
import math

import jax
import jax.numpy as jnp
from jax import lax
from jax.experimental import pallas as pl
from jax.experimental.pallas import tpu as pltpu

CHUNK = 64
SUB = 16
GATE_TEMP = 16.0
EPS = 1e-6
TOP_K = 2
LANES = 128
SUBLANES = 8
BF16_ROWS = 16
VMEM_LIMIT_BYTES = 56 * 1024 * 1024
ROW_DMA_PRIORITY = 1

F32 = jnp.float32
BF16 = jnp.bfloat16


def _pick_tile(n, target, mult):
    best = None
    for d in range(mult, min(n, target) + 1, mult):
        if n % d == 0:
            best = d
    assert best is not None, (n, target, mult)
    return best


def _params(sem):
    return pltpu.CompilerParams(dimension_semantics=sem, vmem_limit_bytes=VMEM_LIMIT_BYTES)


def _dot(a, b):
    return jnp.dot(a, b, preferred_element_type=F32)


def _split3(x):
    hi = x.astype(BF16)
    r1 = x - hi.astype(F32)
    mid = r1.astype(BF16)
    lo = (r1 - mid.astype(F32)).astype(BF16)
    return hi, mid, lo


def _dot_f32(a, b):
    a_hi, a_mid, _ = _split3(a)
    b_hi, b_mid, _ = _split3(b)
    return _dot(a_hi, b_hi) + _dot(a_mid, b_hi) + _dot(a_hi, b_mid)


def _rms(x, g):
    return x * lax.rsqrt(jnp.mean(x * x, axis=-1, keepdims=True) + EPS) * g


def _sigmoid(x):
    return jax.nn.sigmoid(x)


def _log_sigmoid(z):
    return jnp.minimum(z, 0.0) - jnp.log1p(jnp.exp(-jnp.abs(z)))


def _conv_taps(xs, w_ref, cols, off, width):
    sub = lax.broadcasted_iota(jnp.int32, xs[0].shape, 0)
    total = None
    for s in range(SUBLANES):
        acc = None
        for w in range(width):
            a, sw = divmod(off + w, SUBLANES)
            if sw != s:
                continue
            x = xs[a] if s == 0 else jnp.where(sub >= s, xs[a], xs[a + 1])
            term = x * w_ref[w, :, cols]
            acc = term if acc is None else acc + term
        if acc is None:
            continue
        if s:
            acc = pltpu.roll(acc, SUBLANES - s, 0)
        total = acc if total is None else total + acc
    return total


def _ln_swish(y, cb, lng, lnb):
    y = y + cb
    mu = jnp.mean(y, axis=-1, keepdims=True)
    yc = y - mu
    c = yc * lax.rsqrt(jnp.mean(yc * yc, axis=-1, keepdims=True) + EPS) * lng + lnb
    return (c * _sigmoid(c)).astype(BF16)


def _inproj_conv_kernel(x_ref, g_ref, wm_ref, wlr_ref, wa2_ref, ba_ref, cw_ref, cb_ref, lng_ref, lnb_ref,
                        p_ref, lg_ref, c_ref, nh_ref, glu_hbm,
                        nb_ref, buf_ref, sg_ref, sem,
                        *, tm, n_glu, width, hp, n_short_tiles, tiles_per_seq, n_meta, units_per_step):
    i = pl.program_id(0)
    j = pl.program_id(1)
    hist = width - 1
    off = hp - hist
    half = buf_ref.shape[2]
    is_short = i < n_short_tiles
    ip = jnp.maximum(i - n_short_tiles, 0)
    seq_b = ip // tiles_per_seq
    tile_in_seq = ip - seq_b * tiles_per_seq
    n_groups = (off + width - 1) // SUBLANES + 2
    n_units = tm // BF16_ROWS
    n_conv_steps = -(-n_units // units_per_step)

    @pl.when(j == 0)
    def _():
        nb = _rms(x_ref[...], g_ref[...]).astype(BF16)
        nb_ref[...] = nb
        a_lr = _dot(nb, wlr_ref[...])
        z = _dot_f32(a_lr, wa2_ref[...]) + ba_ref[...]
        lg_ref[...] = _log_sigmoid(z) * (1.0 / GATE_TEMP)

        @pl.when(jnp.logical_and(jnp.logical_not(is_short), tile_in_seq == 0))
        def _():
            r0 = pl.multiple_of(seq_b * n_meta, SUBLANES)
            for k in range(n_glu):
                buf_ref[k, 0:hp, :] = jnp.zeros((hp, half), F32)
                buf_ref[k, hp - n_meta:hp, :] = sg_ref[k, pl.ds(r0, n_meta), :]

        @pl.when(jnp.logical_and(jnp.logical_not(is_short), tile_in_seq > 0))
        def _():
            for k in range(n_glu):
                buf_ref[k, 0:hp, :] = buf_ref[k, tm:tm + hp, :]

    @pl.when(j < n_glu)
    def _():
        r = _dot(nb_ref[...], wm_ref[...])
        glu = r[:, :half] * _sigmoid(r[:, half:])

        @pl.when(is_short)
        def _():
            sg_ref[j, pl.ds(pl.multiple_of(i * tm, SUBLANES), tm), :] = glu

        @pl.when(jnp.logical_not(is_short))
        def _():
            buf_ref[j, hp:hp + tm, :] = glu

    @pl.when(jnp.logical_and(j == n_glu - 1, is_short))
    def _():
        rows = pl.ds(pl.multiple_of(i * tm, SUBLANES), tm)
        cp = pltpu.make_async_copy(sg_ref.at[:, rows], glu_hbm.at[:, rows], sem)
        cp.start()
        cp.wait()

    @pl.when(jnp.logical_and(j == n_glu, jnp.logical_and(jnp.logical_not(is_short),
                                                        tile_in_seq == tiles_per_seq - 1)))
    def _():
        for k in range(n_glu):
            nh_ref[0, :, k * half:(k + 1) * half] = buf_ref[k, tm + off:tm + hp, :]

    do_conv = jnp.logical_and(jnp.logical_not(is_short), j - n_glu < n_conv_steps)

    @pl.when(jnp.logical_and(j >= n_glu, do_conv))
    def _():
        groups_per_unit = BF16_ROWS // SUBLANES
        base = pl.multiple_of((j - n_glu) * (units_per_step * BF16_ROWS), BF16_ROWS)
        n_win = units_per_step * groups_per_unit + n_groups - 2
        win = [[buf_ref[k, pl.ds(base + SUBLANES * g, SUBLANES), :] for g in range(n_win)] for k in range(n_glu)]
        chunk = p_ref.shape[1] // units_per_step
        outs = []
        for u in range(units_per_step):
            cols = slice(u * chunk, (u + 1) * chunk)
            p_ref[:, cols] = _dot(nb_ref[...], wm_ref[:, cols])
            ys = []
            for g2 in range(groups_per_unit):
                g0 = u * groups_per_unit + g2
                ys.append(jnp.concatenate(
                    [_conv_taps(win[k][g0:g0 + n_groups - 1], cw_ref, slice(k * half, (k + 1) * half), off, width)
                     for k in range(n_glu)], axis=1))
            outs.append(_ln_swish(jnp.concatenate(ys, axis=0), cb_ref[...], lng_ref[...], lnb_ref[...]))
        for u in range(units_per_step):
            c_ref[pl.ds(base + u * BF16_ROWS, BF16_ROWS), :] = outs[u]

    @pl.when(jnp.logical_and(j >= n_glu, jnp.logical_not(do_conv)))
    def _():
        p_ref[...] = _dot(nb_ref[...], wm_ref[...])


def _inproj_conv(x, g, wm, wlr, wa2, ba, w8, cb, lng, lnb, *, tm, tn, c_dim, n_pm, seq, n_meta):
    n, d = x.shape
    half = tn // 2
    n_glu = c_dim // half
    n_p = wm.shape[1] - 2 * c_dim
    dk = wa2.shape[1]
    width = w8.shape[0]
    hp = -(-(width - 1) // SUBLANES) * SUBLANES
    n_short = n - n_pm
    assert c_dim % half == 0 and n_p % tn == 0 and n_pm % tm == 0 and n_short % tm == 0 and seq % tm == 0
    assert tm % BF16_ROWS == 0 and tm >= hp and n_meta <= width - 1 and n_meta % SUBLANES == 0
    ks, tps, bp = n_short // tm, seq // tm, n_pm // seq
    n_p_steps = n_p // tn
    units_per_step = -(-(tm // BF16_ROWS) // n_p_steps)
    assert (tm // BF16_ROWS) % units_per_step == 0 and tn % units_per_step == 0

    def rb(i):
        return jnp.where(i < ks, n_pm // tm + i, i - ks)

    vec = pl.BlockSpec((1, c_dim), lambda i, j: (0, 0))
    kern = functools.partial(_inproj_conv_kernel, tm=tm, n_glu=n_glu, width=width, hp=hp, n_short_tiles=ks,
                             tiles_per_seq=tps, n_meta=n_meta, units_per_step=units_per_step)
    return pl.pallas_call(
        kern,
        grid=(n // tm, n_glu + n_p_steps),
        in_specs=[
            pl.BlockSpec((tm, d), lambda i, j: (rb(i), 0)),
            pl.BlockSpec((1, d), lambda i, j: (0, 0)),
            pl.BlockSpec((d, tn), lambda i, j: (0, j)),
            pl.BlockSpec((d, LANES), lambda i, j: (0, 0)),
            pl.BlockSpec((LANES, dk), lambda i, j: (0, 0)),
            pl.BlockSpec((1, dk), lambda i, j: (0, 0)),
            pl.BlockSpec((width, SUBLANES, c_dim), lambda i, j: (0, 0, 0)), vec, vec, vec,
        ],
        out_specs=[
            pl.BlockSpec((tm, tn), lambda i, j: (rb(i), jnp.maximum(j - n_glu, 0))),
            pl.BlockSpec((tm, dk), lambda i, j: (rb(i), 0)),
            pl.BlockSpec((tm, c_dim), lambda i, j: (rb(i), 0)),
            pl.BlockSpec((1, width - 1, c_dim), lambda i, j: (jnp.clip((i - ks) // tps, 0, bp - 1), 0, 0)),
            pl.BlockSpec(memory_space=pl.ANY),
        ],
        out_shape=[jax.ShapeDtypeStruct((n, n_p), F32), jax.ShapeDtypeStruct((n, dk), F32),
                   jax.ShapeDtypeStruct((n, c_dim), BF16), jax.ShapeDtypeStruct((bp, width - 1, c_dim), F32),
                   jax.ShapeDtypeStruct((n_glu, n_short, half), F32)],
        scratch_shapes=[pltpu.VMEM((tm, d), BF16), pltpu.VMEM((n_glu, hp + tm, half), F32),
                        pltpu.VMEM((n_glu, n_short, half), F32), pltpu.SemaphoreType.DMA],
        compiler_params=_params(("arbitrary", "arbitrary")),
        name="inproj_conv",
    )(x, g, wm, wlr, wa2, ba, w8, cb, lng, lnb)


def _conv_short_kernel(c_prev, glu_ref, hist_ref, w_ref, cb_ref, lng_ref, lnb_ref, c_ref, nh_ref, buf_ref, *,
                       tt, width, hp):
    del c_prev
    hist = width - 1
    off = hp - hist
    n_glu, _, half = glu_ref.shape
    n_groups = (off + width - 1) // SUBLANES + 2
    for k in range(n_glu):
        buf_ref[k, 0:off, :] = jnp.zeros((off, half), F32)
        buf_ref[k, off:hp, :] = hist_ref[0, :, k * half:(k + 1) * half]
        buf_ref[k, hp:hp + tt, :] = glu_ref[k]
    ys = []
    for r0 in range(0, tt, SUBLANES):
        parts = []
        for k in range(n_glu):
            xs = [buf_ref[k, r0 + SUBLANES * g:r0 + SUBLANES * (g + 1), :] for g in range(n_groups - 1)]
            parts.append(_conv_taps(xs, w_ref, slice(k * half, (k + 1) * half), off, width))
        ys.append(jnp.concatenate(parts, axis=1))
    c_ref[...] = _ln_swish(jnp.concatenate(ys, axis=0), cb_ref[...], lng_ref[...], lnb_ref[...])
    for k in range(n_glu):
        nh_ref[0, :, k * half:(k + 1) * half] = buf_ref[k, tt + off:tt + hp, :]


def _conv_short(c_prev, glu3, hist, w8, cb, lng, lnb, *, row0, tt):
    n, c_dim = c_prev.shape
    n_glu, n_short, half = glu3.shape
    width = w8.shape[0]
    hp = -(-(width - 1) // SUBLANES) * SUBLANES
    nb = n_short // tt
    assert row0 % tt == 0 and tt % BF16_ROWS == 0 and n_short % tt == 0
    rb0 = row0 // tt
    vec = pl.BlockSpec((1, c_dim), lambda s: (0, 0))
    return pl.pallas_call(
        functools.partial(_conv_short_kernel, tt=tt, width=width, hp=hp),
        grid=(nb,),
        in_specs=[
            pl.BlockSpec(memory_space=pl.ANY),
            pl.BlockSpec((n_glu, tt, half), lambda s: (0, s, 0)),
            pl.BlockSpec((1, width - 1, c_dim), lambda s: (s, 0, 0)),
            pl.BlockSpec((width, SUBLANES, c_dim), lambda s: (0, 0, 0)), vec, vec, vec,
        ],
        out_specs=[
            pl.BlockSpec((tt, c_dim), lambda s: (rb0 + s, 0)),
            pl.BlockSpec((1, width - 1, c_dim), lambda s: (s, 0, 0)),
        ],
        out_shape=[jax.ShapeDtypeStruct((n, c_dim), BF16),
                   jax.ShapeDtypeStruct((nb, width - 1, c_dim), F32)],
        scratch_shapes=[pltpu.VMEM((n_glu, hp + tt, half), F32)],
        input_output_aliases={0: 0},
        compiler_params=_params(("parallel",)),
        name="conv_short",
    )(c_prev, glu3, hist, w8, cb, lng, lnb)


def _gla_block(q, k, v, lg, s_ref, c):
    n_heads, dkh, dvh = s_ref.shape
    dk = q.shape[1]
    row = lax.broadcasted_iota(jnp.int32, (c, c), 0)
    col = lax.broadcasted_iota(jnp.int32, (c, c), 1)
    tri = jnp.where(row >= col, 1.0, 0.0).astype(BF16)
    lg3 = _split3(lg)
    b = _dot(tri, lg3[0]) + _dot(tri, lg3[1]) + _dot(tri, lg3[2])
    b_last = b[c - 1:c, :]
    vb = v.astype(BF16)
    qe = (q * jnp.exp(b)).astype(BF16)
    kd = (k * jnp.exp(b_last - b)).astype(BF16)
    tdims = (((0,), (0,)), ((), ()))
    ones = jnp.ones((c, LANES), BF16)
    tot = (lax.dot_general(lg3[0], ones, tdims, preferred_element_type=F32)
           + lax.dot_general(lg3[1], ones, tdims, preferred_element_type=F32)
           + lax.dot_general(lg3[2], ones, tdims, preferred_element_type=F32))
    decay = jnp.exp(tot)

    qis, kis = [], []
    for i in range(c // SUB):
        r0 = i * SUB
        nk = r0 + SUB
        ref = b[r0 - 1:r0, :] if i > 0 else jnp.zeros((1, dk), F32)
        qis.append((q[r0:nk] * jnp.exp(b[r0:nk] - ref)).astype(BF16))
        kis.append((k[0:nk] * jnp.exp(ref - b[0:nk])).astype(BF16))

    kh = lambda x, h: x[:, h * dkh:(h + 1) * dkh]
    vh = lambda x, h: x[:, h * dvh:(h + 1) * dvh]
    heads = range(n_heads)
    o_state = [_dot(kh(qe, h), s_ref[h].astype(BF16)) for h in heads]
    kv = [lax.dot_general(kh(kd, h), vh(vb, h), tdims, preferred_element_type=F32) for h in heads]
    nt = (((1,), (1,)), ((), ()))
    a = [[lax.dot_general(kh(qis[i], h), kh(kis[i], h), nt, preferred_element_type=F32)
          for i in range(c // SUB)] for h in heads]
    outs = []
    for h in heads:
        parts = []
        for i in range(c // SUB):
            r0 = i * SUB
            nk = r0 + SUB
            causal = (lax.broadcasted_iota(jnp.int32, (SUB, nk), 0) + r0
                      >= lax.broadcasted_iota(jnp.int32, (SUB, nk), 1))
            parts.append(_dot(jnp.where(causal, a[h][i], 0.0).astype(BF16), vh(vb, h)[0:nk]))
        outs.append(o_state[h] + (jnp.concatenate(parts, axis=0) if len(parts) > 1 else parts[0]))
    for h in heads:
        dcol = decay[h * dkh:(h + 1) * dkh]
        s_ref[h] = s_ref[h] * jnp.concatenate([dcol] * (dvh // LANES), axis=1) + kv[h]
    return outs


def _gla_kernel(*refs, tt, c, n_heads, scale, has_prev):
    if has_prev:
        refs = refs[1:]
    q_ref, k_ref, v_ref, lg_ref, g_ref, ng_ref, s0_ref, o_ref, so_ref, s_ref = refs
    t = pl.program_id(1)
    dvh = s_ref.shape[2]

    @pl.when(t == 0)
    def _():
        s_ref[...] = s0_ref[0]

    def block(r0):
        rows = pl.ds(r0, c)
        outs = _gla_block(q_ref[rows, :] * scale, k_ref[rows, :], v_ref[rows, :], lg_ref[rows, :], s_ref, c)
        for h in range(n_heads):
            vc = slice(h * dvh, (h + 1) * dvh)
            g = g_ref[rows, vc]
            o_ref[rows, vc] = (_rms(outs[h], ng_ref[...]) * (g * _sigmoid(g))).astype(BF16)

    if tt == c:
        block(0)
    else:
        def body(j, carry):
            block(pl.multiple_of(j * c, c))
            return carry

        lax.fori_loop(0, tt // c, body, 0)

    @pl.when(t == pl.num_programs(1) - 1)
    def _():
        so_ref[0] = s_ref[...]


def _gla_branch(p, lg, ng, s0, *, n, nb, t_len, row0, tt, col_q, col_k, col_v, col_g, n_heads, dkh, dvh,
                prev=None):
    dk, dv = n_heads * dkh, n_heads * dvh
    c = min(CHUNK, t_len)
    assert t_len % tt == 0 and row0 % tt == 0 and tt % c == 0 and c % SUB == 0 and c % BF16_ROWS == 0
    nt = t_len // tt
    rb0 = row0 // tt
    assert col_q % dk == 0 and col_k % dk == 0 and col_v % dv == 0 and col_g % dv == 0
    cq, ck, cv, cg = col_q // dk, col_k // dk, col_v // dv, col_g // dv
    kern = functools.partial(_gla_kernel, tt=tt, c=c, n_heads=n_heads, scale=float(dkh) ** -0.5,
                             has_prev=prev is not None)
    in_specs = [pl.BlockSpec(memory_space=pl.ANY)] if prev is not None else []
    args = [prev] if prev is not None else []
    in_specs += [
        pl.BlockSpec((tt, dk), lambda b, t: (rb0 + b * nt + t, cq)),
        pl.BlockSpec((tt, dk), lambda b, t: (rb0 + b * nt + t, ck)),
        pl.BlockSpec((tt, dv), lambda b, t: (rb0 + b * nt + t, cv)),
        pl.BlockSpec((tt, dk), lambda b, t: (rb0 + b * nt + t, 0)),
        pl.BlockSpec((tt, dv), lambda b, t: (rb0 + b * nt + t, cg)),
        pl.BlockSpec((1, dvh), lambda b, t: (0, 0)),
        pl.BlockSpec((1, n_heads, dkh, dvh), lambda b, t: (b, 0, 0, 0)),
    ]
    args += [p, p, p, lg, p, ng, s0]
    return pl.pallas_call(
        kern,
        grid=(nb, nt),
        in_specs=in_specs,
        out_specs=[
            pl.BlockSpec((tt, dv), lambda b, t: (rb0 + b * nt + t, 0)),
            pl.BlockSpec((1, n_heads, dkh, dvh), lambda b, t: (b, 0, 0, 0)),
        ],
        out_shape=[jax.ShapeDtypeStruct((n, dv), BF16),
                   jax.ShapeDtypeStruct((nb, n_heads, dkh, dvh), F32)],
        scratch_shapes=[pltpu.VMEM((n_heads, dkh, dvh), F32)],
        input_output_aliases={0: 0} if prev is not None else {},
        compiler_params=_params(("parallel", "arbitrary")),
        name="gla_branch",
    )(*args)


def _mix_kernel(c_ref, og_ref, zc_ref, zg_ref, wc_ref, wg_ref, o_ref):
    a = _dot(c_ref[...], wc_ref[...])
    b = _dot(og_ref[...], wg_ref[...])
    o_ref[...] = (_sigmoid(zc_ref[...]) * a + _sigmoid(zg_ref[...]) * b).astype(BF16)


def _mix(c, og, p, wc, wg, col_zc, col_zg, tm, tn):
    n, d = c.shape
    d_out = wc.shape[1]
    assert col_zc % tn == 0 and col_zg % tn == 0
    jc, jg = col_zc // tn, col_zg // tn
    return pl.pallas_call(
        _mix_kernel,
        grid=(n // tm, d_out // tn),
        in_specs=[
            pl.BlockSpec((tm, d), lambda i, j: (i, 0)),
            pl.BlockSpec((tm, og.shape[1]), lambda i, j: (i, 0)),
            pl.BlockSpec((tm, tn), lambda i, j: (i, jc + j)),
            pl.BlockSpec((tm, tn), lambda i, j: (i, jg + j)),
            pl.BlockSpec((d, tn), lambda i, j: (0, j)),
            pl.BlockSpec((og.shape[1], tn), lambda i, j: (0, j)),
        ],
        out_specs=pl.BlockSpec((tm, tn), lambda i, j: (i, j)),
        out_shape=jax.ShapeDtypeStruct((n, d_out), BF16),
        compiler_params=_params(("parallel", "arbitrary")),
        name="mix",
    )(c, og, p, p, wc, wg)


def _resid_mm_kernel(a_ref, w_ref, x_ref, o_ref):
    o_ref[...] = x_ref[...] + _dot(a_ref[...], w_ref[...])


def _resid_mm(a, w, x, tm, tn):
    n, k = a.shape
    d_out = w.shape[1]
    return pl.pallas_call(
        _resid_mm_kernel,
        grid=(n // tm, d_out // tn),
        in_specs=[
            pl.BlockSpec((tm, k), lambda i, j: (i, 0)),
            pl.BlockSpec((k, tn), lambda i, j: (0, j)),
            pl.BlockSpec((tm, tn), lambda i, j: (i, j)),
        ],
        out_specs=pl.BlockSpec((tm, tn), lambda i, j: (i, j)),
        out_shape=jax.ShapeDtypeStruct((n, d_out), F32),
        compiler_params=_params(("parallel", "arbitrary")),
        name="out_proj",
    )(a, w, x)


def _ffn_kernel(x_ref, g_ref, w1_ref, w3_ref, w2_ref, o_ref, hb_ref):
    f = pl.program_id(1)

    @pl.when(f == 0)
    def _():
        x = x_ref[...]
        hb_ref[...] = _rms(x, g_ref[...]).astype(BF16)
        o_ref[...] = x

    h = hb_ref[...]
    a = _dot(h, w1_ref[...])
    b = _dot(h, w3_ref[...])
    t = (a * _sigmoid(a) * b).astype(BF16)
    o_ref[...] += _dot(t, w2_ref[...])


def _ffn(x, g, w1, w3, w2, tm, tf):
    n, d = x.shape
    dff = w1.shape[1]
    return pl.pallas_call(
        _ffn_kernel,
        grid=(n // tm, dff // tf),
        in_specs=[
            pl.BlockSpec((tm, d), lambda i, f: (i, 0)),
            pl.BlockSpec((1, d), lambda i, f: (0, 0)),
            pl.BlockSpec((d, tf), lambda i, f: (0, f)),
            pl.BlockSpec((d, tf), lambda i, f: (0, f)),
            pl.BlockSpec((tf, d), lambda i, f: (f, 0)),
        ],
        out_specs=pl.BlockSpec((tm, d), lambda i, f: (i, 0)),
        out_shape=jax.ShapeDtypeStruct((n, d), F32),
        scratch_shapes=[pltpu.VMEM((tm, d), BF16)],
        compiler_params=_params(("parallel", "arbitrary")),
        name="ffn",
    )(x, g, w1, w3, w2)


def _router_kernel(x_ref, g_ref, rw_ref, idx_ref, gate_ref, *, n_exp):
    h = _rms(x_ref[...], g_ref[...])
    logits = _dot_f32(h, rw_ref[...])
    lane = lax.broadcasted_iota(jnp.int32, logits.shape, 1).astype(F32)
    neg = jnp.float32(-jnp.inf)
    l1 = jnp.where(lane < n_exp, logits, neg)
    m1 = jnp.max(l1, axis=-1, keepdims=True)
    i1 = jnp.min(jnp.where(l1 == m1, lane, float(LANES)), axis=-1, keepdims=True)
    l2 = jnp.where(lane == i1, neg, l1)
    m2 = jnp.max(l2, axis=-1, keepdims=True)
    i2 = jnp.min(jnp.where(l2 == m2, lane, float(LANES)), axis=-1, keepdims=True)
    e = jnp.exp(m2 - m1)
    den = 1.0 + e
    idx_ref[...] = jnp.where(lane == 0, i1, jnp.where(lane == 1, i2, 0.0)).astype(jnp.int32)
    gate_ref[...] = jnp.where(lane == 0, 1.0 / den, jnp.where(lane == 1, e / den, 0.0))


def _router(x, g, rw, n_exp, tm):
    n, d = x.shape
    return pl.pallas_call(
        functools.partial(_router_kernel, n_exp=n_exp),
        grid=(n // tm,),
        in_specs=[
            pl.BlockSpec((tm, d), lambda i: (i, 0)),
            pl.BlockSpec((1, d), lambda i: (0, 0)),
            pl.BlockSpec((d, LANES), lambda i: (0, 0)),
        ],
        out_specs=[pl.BlockSpec((tm, LANES), lambda i: (i, 0)), pl.BlockSpec((tm, LANES), lambda i: (i, 0))],
        out_shape=[jax.ShapeDtypeStruct((n, LANES), jnp.int32), jax.ShapeDtypeStruct((n, LANES), F32)],
        compiler_params=_params(("parallel",)),
        name="router",
    )(x, g, rw)


def _moe_kernel(te_ref, nv_ref, tok_ref, tokn_ref, dst_ref, x_hbm, g_ref, w1_ref, w3_ref, w2_ref, y_hbm,
                xg_ref, hb_ref, acc_ref, tok_s, dst_s, sem_g, sem_s, sem_i, *, tm):
    t = pl.program_id(0)
    f = pl.program_id(1)
    n_tiles = pl.num_programs(0)
    last_f = pl.num_programs(1) - 1
    active = nv_ref[t] > 0
    next_active = jnp.logical_and(t + 1 < n_tiles, nv_ref[jnp.minimum(t + 1, n_tiles - 1)] > 0)

    def start_gather(idx_ref):
        cp = pltpu.make_async_copy(idx_ref.at[0], tok_s, sem_i)
        cp.start()
        cp.wait()

        def issue(r, carry):
            pltpu.make_async_copy(x_hbm.at[pl.ds(tok_s[0, r], 1)], xg_ref.at[pl.ds(r, 1)], sem_g).start(
                priority=ROW_DMA_PRIORITY)
            return carry

        lax.fori_loop(0, tm, issue, 0, unroll=16)

    def wait_gather():
        pltpu.make_async_copy(x_hbm.at[pl.ds(0, tm)], xg_ref, sem_g).wait()

    def start_scatter():
        cp = pltpu.make_async_copy(dst_ref.at[0], dst_s, sem_i)
        cp.start()
        cp.wait()

        def issue(r, carry):
            pltpu.make_async_copy(acc_ref.at[pl.ds(r, 1)], y_hbm.at[pl.ds(dst_s[0, r], 1)], sem_s).start(
                priority=ROW_DMA_PRIORITY)
            return carry

        lax.fori_loop(0, tm, issue, 0, unroll=16)

    def wait_scatter():
        pltpu.make_async_copy(acc_ref, y_hbm.at[pl.ds(0, tm)], sem_s).wait()

    @pl.when(active & (f == 0))
    def _():
        @pl.when(t == 0)
        def _():
            start_gather(tok_ref)

        wait_gather()
        hb_ref[...] = _rms(xg_ref[...], g_ref[...]).astype(BF16)

        @pl.when(next_active)
        def _():
            start_gather(tokn_ref)

        @pl.when(t > 0)
        def _():
            wait_scatter()

    @pl.when(active)
    def _():
        h = hb_ref[...]
        a = _dot(h, w1_ref[0])
        b = _dot(h, w3_ref[0])
        u = (a * _sigmoid(a) * b).astype(BF16)

        @pl.when(f == 0)
        def _():
            acc_ref[...] = _dot(u, w2_ref[0])

        @pl.when(f > 0)
        def _():
            acc_ref[...] += _dot(u, w2_ref[0])

    @pl.when(active & (f == last_f))
    def _():
        start_scatter()

        @pl.when(jnp.logical_not(next_active))
        def _():
            wait_scatter()


def _moe_experts(x, g, w1, w3, w2, tile_e, tile_nv, row_tok, row_dst, n_out_rows, tm, tf):
    n, d = x.shape
    dff = w1.shape[2]
    n_tiles = tile_e.shape[0]
    nf = dff // tf

    def wcol(t, f, te, nv):
        return (te[t], 0, jnp.where(nv[t] > 0, f, nf - 1))

    def wrow(t, f, te, nv):
        return (te[t], jnp.where(nv[t] > 0, f, nf - 1), 0)

    grid_spec = pltpu.PrefetchScalarGridSpec(
        num_scalar_prefetch=2,
        grid=(n_tiles, nf),
        in_specs=[
            pl.BlockSpec((1, 1, tm), lambda t, f, te, nv: (t, 0, 0)),
            pl.BlockSpec((1, 1, tm), lambda t, f, te, nv: (jnp.minimum(t + 1, n_tiles - 1), 0, 0)),
            pl.BlockSpec((1, 1, tm), lambda t, f, te, nv: (t, 0, 0)),
            pl.BlockSpec(memory_space=pl.ANY),
            pl.BlockSpec((1, d), lambda t, f, te, nv: (0, 0)),
            pl.BlockSpec((1, d, tf), wcol),
            pl.BlockSpec((1, d, tf), wcol),
            pl.BlockSpec((1, tf, d), wrow),
        ],
        out_specs=pl.BlockSpec(memory_space=pl.ANY),
        scratch_shapes=[
            pltpu.VMEM((tm, d), F32),
            pltpu.VMEM((tm, d), BF16),
            pltpu.VMEM((tm, d), F32),
            pltpu.SMEM((1, tm), jnp.int32),
            pltpu.SMEM((1, tm), jnp.int32),
            pltpu.SemaphoreType.DMA,
            pltpu.SemaphoreType.DMA,
            pltpu.SemaphoreType.DMA,
        ],
    )
    return pl.pallas_call(
        functools.partial(_moe_kernel, tm=tm),
        grid_spec=grid_spec,
        out_shape=jax.ShapeDtypeStruct((n_out_rows, d), F32),
        compiler_params=_params(("arbitrary", "arbitrary")),
        name="moe_experts",
    )(tile_e, tile_nv, row_tok, row_tok, row_dst, x, g, w1, w3, w2)


def _moe_plan(eidx, n_exp, tm, n_pad):
    n = eidx.shape[0]
    rows = n * TOP_K
    flat_e = eidx.reshape(-1)
    order = jnp.argsort(flat_e, stable=True).astype(jnp.int32)
    sizes = jnp.bincount(flat_e, length=n_exp).astype(jnp.int32)
    starts = jnp.cumsum(sizes) - sizes
    tiles_per = (sizes + tm - 1) // tm
    tile_end = jnp.cumsum(tiles_per)
    tile_start = tile_end - tiles_per
    n_tiles = -(-rows // tm) + n_exp
    t = jnp.arange(n_tiles, dtype=jnp.int32)
    te = jnp.minimum(jnp.searchsorted(tile_end, t, side="right"), n_exp - 1).astype(jnp.int32)
    off = (t - tile_start[te]) * tm
    nv = jnp.where(t < tile_end[-1], jnp.clip(sizes[te] - off, 0, tm), 0).astype(jnp.int32)
    r = jnp.arange(tm, dtype=jnp.int32)[None, :]
    valid = r < nv[:, None]
    flat = order[jnp.clip(starts[te][:, None] + off[:, None] + r, 0, rows - 1)]
    tok = flat // TOP_K
    slot = flat - tok * TOP_K
    row_tok = jnp.where(valid, tok, 0)
    row_dst = jnp.where(valid, slot * n_pad + tok, n + r)
    return te, nv, row_tok[:, None, :], row_dst[:, None, :]


def _combine_kernel(x_ref, y0_ref, y1_ref, gate_ref, g_ref, o_ref, *, final):
    gate = gate_ref[...]
    x = x_ref[...] + (y0_ref[0] * gate[:, 0:1] + y1_ref[0] * gate[:, 1:2])
    o_ref[...] = _rms(x, g_ref[...]) if final else x


def _combine(x, y2, gates, g, tm, final, row0, n_rows):
    d = x.shape[1]
    assert row0 % tm == 0 and n_rows % tm == 0
    i0 = row0 // tm
    return pl.pallas_call(
        functools.partial(_combine_kernel, final=final),
        grid=(n_rows // tm,),
        in_specs=[
            pl.BlockSpec((tm, d), lambda i: (i0 + i, 0)),
            pl.BlockSpec((1, tm, d), lambda i: (0, i0 + i, 0)),
            pl.BlockSpec((1, tm, d), lambda i: (1, i0 + i, 0)),
            pl.BlockSpec((tm, LANES), lambda i: (i0 + i, 0)),
            pl.BlockSpec((1, d), lambda i: (0, 0)),
        ],
        out_specs=pl.BlockSpec((tm, d), lambda i: (i, 0)),
        out_shape=jax.ShapeDtypeStruct((n_rows, d), F32),
        compiler_params=_params(("parallel",)),
        name="combine",
    )(x, y2, y2, gates, g)


def _norm_kernel(x_ref, g_ref, o_ref):
    o_ref[...] = _rms(x_ref[...], g_ref[...])


def _norm(x, g, tm, row0, n_rows):
    d = x.shape[1]
    assert row0 % tm == 0 and n_rows % tm == 0
    i0 = row0 // tm
    return pl.pallas_call(
        _norm_kernel,
        grid=(n_rows // tm,),
        in_specs=[pl.BlockSpec((tm, d), lambda i: (i0 + i, 0)), pl.BlockSpec((1, d), lambda i: (0, 0))],
        out_specs=pl.BlockSpec((tm, d), lambda i: (i, 0)),
        out_shape=jax.ShapeDtypeStruct((n_rows, d), F32),
        compiler_params=_params(("parallel",)),
        name="final_norm",
    )(x, g)


def kernel(x_prompt, x_sample, cache_conv, state_gla, meta_tokens, norm_mix_g, w_in, w_gate_a2, b_gate_a,
           conv_w, conv_b, conv_ln_g, conv_ln_b, w_conv_out, gla_norm_g, w_gla_out, w_out, norm_ffn_g,
           ffn_w1, ffn_w3, ffn_w2, router_w, exp_w1, exp_w3, exp_w2, final_norm_g):
    bp, seq, d = x_prompt.shape
    bs, ts, _ = x_sample.shape
    n_meta = meta_tokens.shape[0]
    depth = w_in.shape[0]
    c_dim = conv_w.shape[2]
    width = conv_w.shape[1]
    n_heads, dkh, dvh = state_gla.shape[2:]
    dk, dv = n_heads * dkh, n_heads * dvh
    rank = w_gate_a2.shape[1]
    n_exp = router_w.shape[2]
    assert rank <= LANES and n_exp <= LANES

    assert n_meta == ts, "meta and sample blocks share the short-sequence calls"
    n_pm, n_mt, n_s = bp * seq, bp * n_meta, bs * ts
    row_short, row_s = n_pm, n_pm + n_mt
    n = n_pm + n_mt + n_s
    meta = jnp.broadcast_to(meta_tokens[None].astype(x_prompt.dtype), (bp, n_meta, d))
    x = jnp.concatenate([x_prompt.reshape(n_pm, d), meta.reshape(n_mt, d), x_sample.reshape(n_s, d)], axis=0)

    in_q = 2 * c_dim
    in_alr = in_q + 2 * dk + 2 * dv
    col_q, col_k, col_v, col_g = 0, dk, 2 * dk, 2 * dk + dv
    col_zc = col_g + dv
    col_zg = col_zc + d
    n_p = col_zg + d

    tm_big = _pick_tile(n, 1280, BF16_ROWS)
    tm_mid = _pick_tile(n, 640, BF16_ROWS)
    tn_in = _pick_tile(math.gcd(n_p, 2 * c_dim), 1024, 2 * LANES)
    tm_in = _pick_tile(math.gcd(seq, n_mt + n_s), 512, BF16_ROWS)
    tn_d = _pick_tile(d, 512, LANES)
    tt_gla = _pick_tile(seq, 256, CHUNK)
    tm_out_p = _pick_tile(n_pm, 640, BF16_ROWS)
    tm_out_s = _pick_tile(math.gcd(row_s, n_s), 640, BF16_ROWS)
    row = lambda v: v.reshape(1, -1).astype(F32)

    new_conv_p, new_conv_s, new_gla_p, new_gla_s = [], [], [], []
    for i in range(depth):
        wi = w_in[i]
        half = tn_in // 2
        glu_cols = [wi[:, o + k * half:o + (k + 1) * half] for k in range(c_dim // half) for o in (0, c_dim)]
        wm = jnp.concatenate(glu_cols + [wi[:, in_q:in_alr], wi[:, in_alr + rank:]], axis=1).astype(BF16)
        wlr = jnp.pad(wi[:, in_alr:in_alr + rank], ((0, 0), (0, LANES - rank))).astype(BF16)
        wa2 = jnp.pad(w_gate_a2[i], ((0, LANES - rank), (0, 0))).astype(F32)
        w8 = jnp.broadcast_to(conv_w[i][:, None, :], (width, SUBLANES, c_dim)).astype(F32)
        cargs = (w8, row(conv_b[i]), row(conv_ln_g[i]), row(conv_ln_b[i]))
        p, lg, c_p, nc_p, glu_short = _inproj_conv(
            x, row(norm_mix_g[i]), wm, wlr, wa2, row(b_gate_a[i]), *cargs,
            tm=tm_in, tn=tn_in, c_dim=c_dim, n_pm=n_pm, seq=seq, n_meta=n_meta)
        hist0 = jnp.concatenate([jnp.zeros((bp,) + cache_conv.shape[2:], F32), cache_conv[i].astype(F32)], axis=0)
        c, nh_short = _conv_short(c_p, glu_short, hist0, *cargs, row0=row_short, tt=ts)
        nc_s = nh_short[bp:]

        gargs = dict(n=n, col_q=col_q, col_k=col_k, col_v=col_v, col_g=col_g, n_heads=n_heads, dkh=dkh, dvh=dvh)
        ng = row(gla_norm_g[i])
        s0 = jnp.concatenate([jnp.zeros((bp,) + state_gla.shape[2:], F32), state_gla[i].astype(F32)], axis=0)
        og_short, st_short = _gla_branch(p, lg, ng, s0, nb=bp + bs, t_len=ts, row0=row_short, tt=ts, **gargs)
        og, ns_p = _gla_branch(p, lg, ng, st_short, nb=bp, t_len=seq, row0=0, tt=tt_gla, prev=og_short, **gargs)
        ns_s = st_short[bp:]

        mixed = _mix(c, og, p, w_conv_out[i].astype(BF16), w_gla_out[i].astype(BF16), col_zc, col_zg,
                     tm_big, tn_d)
        x = _resid_mm(mixed, w_out[i].astype(BF16), x, tm_mid, d)

        j = i // 2
        last = i == depth - 1
        if i % 2 == 0:
            tf = _pick_tile(ffn_w1.shape[2], 512, LANES)
            x = _ffn(x, row(norm_ffn_g[i]), ffn_w1[j].astype(BF16), ffn_w3[j].astype(BF16),
                     ffn_w2[j].astype(BF16), tm_mid, tf)
            if last:
                y_p = _norm(x, row(final_norm_g), tm_out_p, 0, n_pm)
                y_s = _norm(x, row(final_norm_g), tm_out_s, row_s, n_s)
        else:
            rw = jnp.pad(router_w[j], ((0, 0), (0, LANES - n_exp))).astype(F32)
            eidx, gates = _router(x, row(norm_ffn_g[i]), rw, n_exp, tm_mid)
            tm_e = 1024 if n * TOP_K >= 8192 else 64
            n_pad = n + tm_e
            te, nv, row_tok, row_dst = _moe_plan(eidx[:, :TOP_K], n_exp, tm_e, n_pad)
            tf = _pick_tile(exp_w1.shape[3], 512, LANES)
            y2 = _moe_experts(x, row(norm_ffn_g[i]), exp_w1[j].astype(BF16), exp_w3[j].astype(BF16),
                              exp_w2[j].astype(BF16), te, nv, row_tok, row_dst, TOP_K * n_pad, tm_e, tf)
            y2 = y2.reshape(TOP_K, n_pad, d)
            if last:
                y_p = _combine(x, y2, gates, row(final_norm_g), tm_out_p, True, 0, n_pm)
                y_s = _combine(x, y2, gates, row(final_norm_g), tm_out_s, True, row_s, n_s)
            else:
                x = _combine(x, y2, gates, row(final_norm_g), tm_mid, False, 0, n)
        new_conv_p.append(nc_p)
        new_conv_s.append(nc_s)
        new_gla_p.append(ns_p)
        new_gla_s.append(ns_s)

    return (y_p.reshape(bp, seq, d), y_s.reshape(bs, ts, d), jnp.stack(new_conv_p), jnp.stack(new_gla_p),
            jnp.stack(new_conv_s), jnp.stack(new_gla_s))
```

```python
import functools
import math

import jax
import jax.numpy as jnp
from jax import lax
from jax.experimental import pallas as pl
from jax.experimental.pallas import tpu as pltpu

CHUNK = 64
SUB = 16
GATE_TEMP = 16.0
EPS = 1e-6
TOP_K = 2
LANES = 128
SUBLANES = 8
BF16_ROWS = 16
VMEM_LIMIT_BYTES = 56 * 1024 * 1024
ROW_DMA_PRIORITY = 1
ROW_DMA_BLOCK = 32

F32 = jnp.float32
BF16 = jnp.bfloat16


def _pick_tile(n, target, mult):
    best = None
    for d in range(mult, min(n, target) + 1, mult):
        if n % d == 0:
            best = d
    assert best is not None, (n, target, mult)
    return best


def _params(sem):
    return pltpu.CompilerParams(dimension_semantics=sem, vmem_limit_bytes=VMEM_LIMIT_BYTES)


def _dot(a, b):
    return jnp.dot(a, b, preferred_element_type=F32)


def _split3(x):
    hi = x.astype(BF16)
    r1 = x - hi.astype(F32)
    mid = r1.astype(BF16)
    lo = (r1 - mid.astype(F32)).astype(BF16)
    return hi, mid, lo


def _dot_f32(a, b):
    a_hi, a_mid, _ = _split3(a)
    b_hi, b_mid, _ = _split3(b)
    return _dot(a_hi, b_hi) + _dot(a_mid, b_hi) + _dot(a_hi, b_mid)


def _rms(x, g):
    return x * lax.rsqrt(jnp.mean(x * x, axis=-1, keepdims=True) + EPS) * g


def _sigmoid(x):
    return jax.nn.sigmoid(x)


def _log_sigmoid(z):
    return jnp.minimum(z, 0.0) - jnp.log1p(jnp.exp(-jnp.abs(z)))


def _conv_taps(xs, w_ref, cols, off, width):
    return _conv_groups(xs, w_ref, cols, off, width, 1)[0]


def _conv_groups(xs, w_ref, cols, off, width, n_out):
    sub = lax.broadcasted_iota(jnp.int32, xs[0].shape, 0)
    totals = [None] * n_out
    for s in range(SUBLANES):
        taps = [(w, (off + w) // SUBLANES) for w in range(width) if (off + w) % SUBLANES == s]
        if not taps:
            continue
        shifted = {}
        for g in range(n_out):
            acc = None
            for w, a in taps:
                m = g + a
                if m not in shifted:
                    shifted[m] = xs[m] if s == 0 else jnp.where(sub >= s, xs[m], xs[m + 1])
                term = shifted[m] * w_ref[w, :, cols]
                acc = term if acc is None else acc + term
            if s:
                acc = pltpu.roll(acc, SUBLANES - s, 0)
            totals[g] = acc if totals[g] is None else totals[g] + acc
    return totals


def _ln_swish(y, cb, lng, lnb):
    y = y + cb
    mu = jnp.mean(y, axis=-1, keepdims=True)
    yc = y - mu
    c = yc * lax.rsqrt(jnp.mean(yc * yc, axis=-1, keepdims=True) + EPS) * lng + lnb
    return (c * _sigmoid(c)).astype(BF16)


def _inproj_conv_kernel(x_ref, g_ref, wm_ref, wlr_ref, wa2_ref, ba_ref, cw_ref, cb_ref, lng_ref, lnb_ref,
                        p_ref, lg_ref, c_ref, nh_ref, glu_hbm,
                        nb_ref, buf_ref, sg_ref, sem,
                        *, tm, n_glu, width, hp, n_short_tiles, tiles_per_seq, n_meta, units_per_step):
    i = pl.program_id(0)
    j = pl.program_id(1)
    hist = width - 1
    off = hp - hist
    half = buf_ref.shape[2]
    is_short = i < n_short_tiles
    ip = jnp.maximum(i - n_short_tiles, 0)
    seq_b = ip // tiles_per_seq
    tile_in_seq = ip - seq_b * tiles_per_seq
    n_groups = (off + width - 1) // SUBLANES + 2
    n_units = tm // BF16_ROWS
    n_conv_steps = -(-n_units // units_per_step)

    @pl.when(j == 0)
    def _():
        nb = _rms(x_ref[...], g_ref[...]).astype(BF16)
        nb_ref[...] = nb
        a_lr = _dot(nb, wlr_ref[...])
        z = _dot_f32(a_lr, wa2_ref[...]) + ba_ref[...]
        lg_ref[...] = _log_sigmoid(z) * (1.0 / GATE_TEMP)

        @pl.when(jnp.logical_and(jnp.logical_not(is_short), tile_in_seq == 0))
        def _():
            r0 = pl.multiple_of(seq_b * n_meta, SUBLANES)
            for k in range(n_glu):
                buf_ref[k, 0:hp, :] = jnp.zeros((hp, half), F32)
                buf_ref[k, hp - n_meta:hp, :] = sg_ref[k, pl.ds(r0, n_meta), :]

        @pl.when(jnp.logical_and(jnp.logical_not(is_short), tile_in_seq > 0))
        def _():
            for k in range(n_glu):
                buf_ref[k, 0:hp, :] = buf_ref[k, tm:tm + hp, :]

    @pl.when(j < n_glu)
    def _():
        r = _dot(nb_ref[...], wm_ref[...])
        glu = r[:, :half] * _sigmoid(r[:, half:])

        @pl.when(is_short)
        def _():
            sg_ref[j, pl.ds(pl.multiple_of(i * tm, SUBLANES), tm), :] = glu

        @pl.when(jnp.logical_not(is_short))
        def _():
            buf_ref[j, hp:hp + tm, :] = glu

    @pl.when(jnp.logical_and(j == n_glu - 1, is_short))
    def _():
        rows = pl.ds(pl.multiple_of(i * tm, SUBLANES), tm)
        cp = pltpu.make_async_copy(sg_ref.at[:, rows], glu_hbm.at[:, rows], sem)
        cp.start()
        cp.wait()

    @pl.when(jnp.logical_and(j == n_glu, jnp.logical_and(jnp.logical_not(is_short),
                                                        tile_in_seq == tiles_per_seq - 1)))
    def _():
        for k in range(n_glu):
            nh_ref[0, :, k * half:(k + 1) * half] = buf_ref[k, tm + off:tm + hp, :]

    do_conv = jnp.logical_and(jnp.logical_not(is_short), j - n_glu < n_conv_steps)

    @pl.when(jnp.logical_and(j >= n_glu, do_conv))
    def _():
        p_ref[...] = _dot(nb_ref[...], wm_ref[...])
        groups_per_unit = BF16_ROWS // SUBLANES
        n_out = units_per_step * groups_per_unit
        base = pl.multiple_of((j - n_glu) * (units_per_step * BF16_ROWS), BF16_ROWS)
        ys = []
        for k in range(n_glu):
            win = [buf_ref[k, pl.ds(base + SUBLANES * g, SUBLANES), :] for g in range(n_out + n_groups - 2)]
            ys.append(_conv_groups(win, cw_ref, slice(k * half, (k + 1) * half), off, width, n_out))
        for u in range(units_per_step):
            y = jnp.concatenate([jnp.concatenate([ys[k][u * groups_per_unit + g2] for k in range(n_glu)], axis=1)
                                 for g2 in range(groups_per_unit)], axis=0)
            c_ref[pl.ds(base + u * BF16_ROWS, BF16_ROWS), :] = _ln_swish(y, cb_ref[...], lng_ref[...], lnb_ref[...])

    @pl.when(jnp.logical_and(j >= n_glu, jnp.logical_not(do_conv)))
    def _():
        p_ref[...] = _dot(nb_ref[...], wm_ref[...])


def _inproj_conv(x, g, wm, wlr, wa2, ba, w8, cb, lng, lnb, *, tm, tn, c_dim, n_pm, seq, n_meta):
    n, d = x.shape
    half = tn // 2
    n_glu = c_dim // half
    n_p = wm.shape[1] - 2 * c_dim
    dk = wa2.shape[1]
    width = w8.shape[0]
    hp = -(-(width - 1) // SUBLANES) * SUBLANES
    n_short = n - n_pm
    assert c_dim % half == 0 and n_p % tn == 0 and n_pm % tm == 0 and n_short % tm == 0 and seq % tm == 0
    assert tm % BF16_ROWS == 0 and tm >= hp and n_meta <= width - 1 and n_meta % SUBLANES == 0
    ks, tps, bp = n_short // tm, seq // tm, n_pm // seq
    n_p_steps = n_p // tn
    units_per_step = -(-(tm // BF16_ROWS) // n_p_steps)
    assert (tm // BF16_ROWS) % units_per_step == 0 and tn % units_per_step == 0

    def rb(i):
        return jnp.where(i < ks, n_pm // tm + i, i - ks)

    vec = pl.BlockSpec((1, c_dim), lambda i, j: (0, 0))
    kern = functools.partial(_inproj_conv_kernel, tm=tm, n_glu=n_glu, width=width, hp=hp, n_short_tiles=ks,
                             tiles_per_seq=tps, n_meta=n_meta, units_per_step=units_per_step)
    return pl.pallas_call(
        kern,
        grid=(n // tm, n_glu + n_p_steps),
        in_specs=[
            pl.BlockSpec((tm, d), lambda i, j: (rb(i), 0)),
            pl.BlockSpec((1, d), lambda i, j: (0, 0)),
            pl.BlockSpec((d, tn), lambda i, j: (0, j)),
            pl.BlockSpec((d, LANES), lambda i, j: (0, 0)),
            pl.BlockSpec((LANES, dk), lambda i, j: (0, 0)),
            pl.BlockSpec((1, dk), lambda i, j: (0, 0)),
            pl.BlockSpec((width, SUBLANES, c_dim), lambda i, j: (0, 0, 0)), vec, vec, vec,
        ],
        out_specs=[
            pl.BlockSpec((tm, tn), lambda i, j: (rb(i), jnp.maximum(j - n_glu, 0))),
            pl.BlockSpec((tm, dk), lambda i, j: (rb(i), 0)),
            pl.BlockSpec((tm, c_dim), lambda i, j: (rb(i), 0)),
            pl.BlockSpec((1, width - 1, c_dim), lambda i, j: (jnp.clip((i - ks) // tps, 0, bp - 1), 0, 0)),
            pl.BlockSpec(memory_space=pl.ANY),
        ],
        out_shape=[jax.ShapeDtypeStruct((n, n_p), F32), jax.ShapeDtypeStruct((n, dk), F32),
                   jax.ShapeDtypeStruct((n, c_dim), BF16), jax.ShapeDtypeStruct((bp, width - 1, c_dim), F32),
                   jax.ShapeDtypeStruct((n_glu, n_short, half), F32)],
        scratch_shapes=[pltpu.VMEM((tm, d), BF16), pltpu.VMEM((n_glu, hp + tm, half), F32),
                        pltpu.VMEM((n_glu, n_short, half), F32), pltpu.SemaphoreType.DMA],
        compiler_params=_params(("arbitrary", "arbitrary")),
        name="inproj_conv",
    )(x, g, wm, wlr, wa2, ba, w8, cb, lng, lnb)


def _conv_short_kernel(c_prev, glu_ref, hist_ref, w_ref, cb_ref, lng_ref, lnb_ref, c_ref, nh_ref, buf_ref, *,
                       tt, width, hp):
    del c_prev
    hist = width - 1
    off = hp - hist
    n_glu, _, half = glu_ref.shape
    n_groups = (off + width - 1) // SUBLANES + 2
    for k in range(n_glu):
        buf_ref[k, 0:off, :] = jnp.zeros((off, half), F32)
        buf_ref[k, off:hp, :] = hist_ref[0, :, k * half:(k + 1) * half]
        buf_ref[k, hp:hp + tt, :] = glu_ref[k]
    ys = []
    for r0 in range(0, tt, SUBLANES):
        parts = []
        for k in range(n_glu):
            xs = [buf_ref[k, r0 + SUBLANES * g:r0 + SUBLANES * (g + 1), :] for g in range(n_groups - 1)]
            parts.append(_conv_taps(xs, w_ref, slice(k * half, (k + 1) * half), off, width))
        ys.append(jnp.concatenate(parts, axis=1))
    c_ref[...] = _ln_swish(jnp.concatenate(ys, axis=0), cb_ref[...], lng_ref[...], lnb_ref[...])
    for k in range(n_glu):
        nh_ref[0, :, k * half:(k + 1) * half] = buf_ref[k, tt + off:tt + hp, :]


def _conv_short(c_prev, glu3, hist, w8, cb, lng, lnb, *, row0, tt):
    n, c_dim = c_prev.shape
    n_glu, n_short, half = glu3.shape
    width = w8.shape[0]
    hp = -(-(width - 1) // SUBLANES) * SUBLANES
    nb = n_short // tt
    assert row0 % tt == 0 and tt % BF16_ROWS == 0 and n_short % tt == 0
    rb0 = row0 // tt
    vec = pl.BlockSpec((1, c_dim), lambda s: (0, 0))
    return pl.pallas_call(
        functools.partial(_conv_short_kernel, tt=tt, width=width, hp=hp),
        grid=(nb,),
        in_specs=[
            pl.BlockSpec(memory_space=pl.ANY),
            pl.BlockSpec((n_glu, tt, half), lambda s: (0, s, 0)),
            pl.BlockSpec((1, width - 1, c_dim), lambda s: (s, 0, 0)),
            pl.BlockSpec((width, SUBLANES, c_dim), lambda s: (0, 0, 0)), vec, vec, vec,
        ],
        out_specs=[
            pl.BlockSpec((tt, c_dim), lambda s: (rb0 + s, 0)),
            pl.BlockSpec((1, width - 1, c_dim), lambda s: (s, 0, 0)),
        ],
        out_shape=[jax.ShapeDtypeStruct((n, c_dim), BF16),
                   jax.ShapeDtypeStruct((nb, width - 1, c_dim), F32)],
        scratch_shapes=[pltpu.VMEM((n_glu, hp + tt, half), F32)],
        input_output_aliases={0: 0},
        compiler_params=_params(("parallel",)),
        name="conv_short",
    )(c_prev, glu3, hist, w8, cb, lng, lnb)


def _gla_block(q, k, v, lg, s_ref, c):
    n_heads, dkh, dvh = s_ref.shape
    dk = q.shape[1]
    row = lax.broadcasted_iota(jnp.int32, (c, c), 0)
    col = lax.broadcasted_iota(jnp.int32, (c, c), 1)
    tri = jnp.where(row >= col, 1.0, 0.0).astype(BF16)
    lg3 = _split3(lg)
    b = _dot(tri, lg3[0]) + _dot(tri, lg3[1]) + _dot(tri, lg3[2])
    b_last = b[c - 1:c, :]
    vb = v.astype(BF16)
    qe = (q * jnp.exp(b)).astype(BF16)
    kd = (k * jnp.exp(b_last - b)).astype(BF16)
    tdims = (((0,), (0,)), ((), ()))
    decay = jnp.exp(jnp.transpose(jnp.broadcast_to(b_last, (SUBLANES, dk)))[:, 0:1])

    qis, kis = [], []
    for i in range(c // SUB):
        r0 = i * SUB
        nk = r0 + SUB
        ref = b[r0 - 1:r0, :] if i > 0 else jnp.zeros((1, dk), F32)
        qis.append((q[r0:nk] * jnp.exp(b[r0:nk] - ref)).astype(BF16))
        kis.append((k[0:nk] * jnp.exp(ref - b[0:nk])).astype(BF16))

    kh = lambda x, h: x[:, h * dkh:(h + 1) * dkh]
    vh = lambda x, h: x[:, h * dvh:(h + 1) * dvh]
    heads = range(n_heads)
    o_state = [_dot(kh(qe, h), s_ref[h].astype(BF16)) for h in heads]
    kv = [lax.dot_general(kh(kd, h), vh(vb, h), tdims, preferred_element_type=F32) for h in heads]
    nt = (((1,), (1,)), ((), ()))
    a = [[lax.dot_general(kh(qis[i], h), kh(kis[i], h), nt, preferred_element_type=F32)
          for i in range(c // SUB)] for h in heads]
    outs = []
    for h in heads:
        parts = []
        for i in range(c // SUB):
            r0 = i * SUB
            nk = r0 + SUB
            causal = (lax.broadcasted_iota(jnp.int32, (SUB, nk), 0) + r0
                      >= lax.broadcasted_iota(jnp.int32, (SUB, nk), 1))
            parts.append(_dot(jnp.where(causal, a[h][i], 0.0).astype(BF16), vh(vb, h)[0:nk]))
        outs.append(o_state[h] + (jnp.concatenate(parts, axis=0) if len(parts) > 1 else parts[0]))
    for h in heads:
        s_ref[h] = s_ref[h] * decay[h * dkh:(h + 1) * dkh] + kv[h]
    return outs


def _gla_kernel(*refs, tt, c, n_heads, scale, has_prev):
    if has_prev:
        refs = refs[1:]
    q_ref, k_ref, v_ref, lg_ref, g_ref, ng_ref, s0_ref, o_ref, so_ref, s_ref = refs
    t = pl.program_id(1)
    dvh = s_ref.shape[2]

    @pl.when(t == 0)
    def _():
        s_ref[...] = s0_ref[0]

    def block(r0):
        rows = pl.ds(r0, c)
        outs = _gla_block(q_ref[rows, :] * scale, k_ref[rows, :], v_ref[rows, :], lg_ref[rows, :], s_ref, c)
        for h in range(n_heads):
            vc = slice(h * dvh, (h + 1) * dvh)
            g = g_ref[rows, vc]
            o_ref[rows, vc] = (_rms(outs[h], ng_ref[...]) * (g * _sigmoid(g))).astype(BF16)

    if tt == c:
        block(0)
    else:
        def body(j, carry):
            block(pl.multiple_of(j * c, c))
            return carry

        lax.fori_loop(0, tt // c, body, 0)

    @pl.when(t == pl.num_programs(1) - 1)
    def _():
        so_ref[0] = s_ref[...]


def _gla_branch(p, lg, ng, s0, *, n, nb, t_len, row0, tt, col_q, col_k, col_v, col_g, n_heads, dkh, dvh,
                prev=None):
    dk, dv = n_heads * dkh, n_heads * dvh
    c = min(CHUNK, t_len)
    assert t_len % tt == 0 and row0 % tt == 0 and tt % c == 0 and c % SUB == 0 and c % BF16_ROWS == 0
    nt = t_len // tt
    rb0 = row0 // tt
    assert col_q % dk == 0 and col_k % dk == 0 and col_v % dv == 0 and col_g % dv == 0
    cq, ck, cv, cg = col_q // dk, col_k // dk, col_v // dv, col_g // dv
    kern = functools.partial(_gla_kernel, tt=tt, c=c, n_heads=n_heads, scale=float(dkh) ** -0.5,
                             has_prev=prev is not None)
    in_specs = [pl.BlockSpec(memory_space=pl.ANY)] if prev is not None else []
    args = [prev] if prev is not None else []
    in_specs += [
        pl.BlockSpec((tt, dk), lambda b, t: (rb0 + b * nt + t, cq)),
        pl.BlockSpec((tt, dk), lambda b, t: (rb0 + b * nt + t, ck)),
        pl.BlockSpec((tt, dv), lambda b, t: (rb0 + b * nt + t, cv)),
        pl.BlockSpec((tt, dk), lambda b, t: (rb0 + b * nt + t, 0)),
        pl.BlockSpec((tt, dv), lambda b, t: (rb0 + b * nt + t, cg)),
        pl.BlockSpec((1, dvh), lambda b, t: (0, 0)),
        pl.BlockSpec((1, n_heads, dkh, dvh), lambda b, t: (b, 0, 0, 0)),
    ]
    args += [p, p, p, lg, p, ng, s0]
    return pl.pallas_call(
        kern,
        grid=(nb, nt),
        in_specs=in_specs,
        out_specs=[
            pl.BlockSpec((tt, dv), lambda b, t: (rb0 + b * nt + t, 0)),
            pl.BlockSpec((1, n_heads, dkh, dvh), lambda b, t: (b, 0, 0, 0)),
        ],
        out_shape=[jax.ShapeDtypeStruct((n, dv), BF16),
                   jax.ShapeDtypeStruct((nb, n_heads, dkh, dvh), F32)],
        scratch_shapes=[pltpu.VMEM((n_heads, dkh, dvh), F32)],
        input_output_aliases={0: 0} if prev is not None else {},
        compiler_params=_params(("parallel", "arbitrary")),
        name="gla_branch",
    )(*args)


def _mix_kernel(c_ref, og_ref, zc_ref, zg_ref, wc_ref, wg_ref, o_ref):
    a = _dot(c_ref[...], wc_ref[...])
    b = _dot(og_ref[...], wg_ref[...])
    o_ref[...] = (_sigmoid(zc_ref[...]) * a + _sigmoid(zg_ref[...]) * b).astype(BF16)


def _mix(c, og, p, wc, wg, col_zc, col_zg, tm, tn):
    n, d = c.shape
    d_out = wc.shape[1]
    assert col_zc % tn == 0 and col_zg % tn == 0
    jc, jg = col_zc // tn, col_zg // tn
    return pl.pallas_call(
        _mix_kernel,
        grid=(n // tm, d_out // tn),
        in_specs=[
            pl.BlockSpec((tm, d), lambda i, j: (i, 0)),
            pl.BlockSpec((tm, og.shape[1]), lambda i, j: (i, 0)),
            pl.BlockSpec((tm, tn), lambda i, j: (i, jc + j)),
            pl.BlockSpec((tm, tn), lambda i, j: (i, jg + j)),
            pl.BlockSpec((d, tn), lambda i, j: (0, j)),
            pl.BlockSpec((og.shape[1], tn), lambda i, j: (0, j)),
        ],
        out_specs=pl.BlockSpec((tm, tn), lambda i, j: (i, j)),
        out_shape=jax.ShapeDtypeStruct((n, d_out), BF16),
        compiler_params=_params(("parallel", "arbitrary")),
        name="mix",
    )(c, og, p, p, wc, wg)


def _resid_mm_kernel(a_ref, w_ref, x_ref, o_ref):
    o_ref[...] = x_ref[...] + _dot(a_ref[...], w_ref[...])


def _resid_mm(a, w, x, tm, tn):
    n, k = a.shape
    d_out = w.shape[1]
    return pl.pallas_call(
        _resid_mm_kernel,
        grid=(n // tm, d_out // tn),
        in_specs=[
            pl.BlockSpec((tm, k), lambda i, j: (i, 0)),
            pl.BlockSpec((k, tn), lambda i, j: (0, j)),
            pl.BlockSpec((tm, tn), lambda i, j: (i, j)),
        ],
        out_specs=pl.BlockSpec((tm, tn), lambda i, j: (i, j)),
        out_shape=jax.ShapeDtypeStruct((n, d_out), F32),
        compiler_params=_params(("parallel", "arbitrary")),
        name="out_proj",
    )(a, w, x)


def _ffn_kernel(x_ref, g_ref, w1_ref, w3_ref, w2_ref, o_ref, hb_ref):
    f = pl.program_id(1)

    @pl.when(f == 0)
    def _():
        x = x_ref[...]
        hb_ref[...] = _rms(x, g_ref[...]).astype(BF16)
        o_ref[...] = x

    h = hb_ref[...]
    a = _dot(h, w1_ref[...])
    b = _dot(h, w3_ref[...])
    t = (a * _sigmoid(a) * b).astype(BF16)
    o_ref[...] += _dot(t, w2_ref[...])


def _ffn(x, g, w1, w3, w2, tm, tf):
    n, d = x.shape
    dff = w1.shape[1]
    return pl.pallas_call(
        _ffn_kernel,
        grid=(n // tm, dff // tf),
        in_specs=[
            pl.BlockSpec((tm, d), lambda i, f: (i, 0)),
            pl.BlockSpec((1, d), lambda i, f: (0, 0)),
            pl.BlockSpec((d, tf), lambda i, f: (0, f)),
            pl.BlockSpec((d, tf), lambda i, f: (0, f)),
            pl.BlockSpec((tf, d), lambda i, f: (f, 0)),
        ],
        out_specs=pl.BlockSpec((tm, d), lambda i, f: (i, 0)),
        out_shape=jax.ShapeDtypeStruct((n, d), F32),
        scratch_shapes=[pltpu.VMEM((tm, d), BF16)],
        compiler_params=_params(("parallel", "arbitrary")),
        name="ffn",
    )(x, g, w1, w3, w2)


def _router_kernel(x_ref, g_ref, rw_ref, idx_ref, gate_ref, *, n_exp):
    h = _rms(x_ref[...], g_ref[...])
    logits = _dot_f32(h, rw_ref[...])
    lane = lax.broadcasted_iota(jnp.int32, logits.shape, 1).astype(F32)
    neg = jnp.float32(-jnp.inf)
    l1 = jnp.where(lane < n_exp, logits, neg)
    m1 = jnp.max(l1, axis=-1, keepdims=True)
    i1 = jnp.min(jnp.where(l1 == m1, lane, float(LANES)), axis=-1, keepdims=True)
    l2 = jnp.where(lane == i1, neg, l1)
    m2 = jnp.max(l2, axis=-1, keepdims=True)
    i2 = jnp.min(jnp.where(l2 == m2, lane, float(LANES)), axis=-1, keepdims=True)
    e = jnp.exp(m2 - m1)
    den = 1.0 + e
    idx_ref[...] = jnp.where(lane == 0, i1, jnp.where(lane == 1, i2, 0.0)).astype(jnp.int32)
    gate_ref[...] = jnp.where(lane == 0, 1.0 / den, jnp.where(lane == 1, e / den, 0.0))


def _router(x, g, rw, n_exp, tm):
    n, d = x.shape
    return pl.pallas_call(
        functools.partial(_router_kernel, n_exp=n_exp),
        grid=(n // tm,),
        in_specs=[
            pl.BlockSpec((tm, d), lambda i: (i, 0)),
            pl.BlockSpec((1, d), lambda i: (0, 0)),
            pl.BlockSpec((d, LANES), lambda i: (0, 0)),
        ],
        out_specs=[pl.BlockSpec((tm, LANES), lambda i: (i, 0)), pl.BlockSpec((tm, LANES), lambda i: (i, 0))],
        out_shape=[jax.ShapeDtypeStruct((n, LANES), jnp.int32), jax.ShapeDtypeStruct((n, LANES), F32)],
        compiler_params=_params(("parallel",)),
        name="router",
    )(x, g, rw)


def _moe_kernel(te_ref, nv_ref, tok_ref, tokn_ref, dst_ref, x_hbm, g_ref, w1_ref, w3_ref, w2_ref, y_hbm,
                xg_ref, hb_ref, acc_ref, tok_s, dst_s, sem_g, sem_s, sem_i, *, tm):
    t = pl.program_id(0)
    f = pl.program_id(1)
    n_tiles = pl.num_programs(0)
    last_f = pl.num_programs(1) - 1
    active = nv_ref[t] > 0
    next_active = jnp.logical_and(t + 1 < n_tiles, nv_ref[jnp.minimum(t + 1, n_tiles - 1)] > 0)

    def start_gather(idx_ref):
        cp = pltpu.make_async_copy(idx_ref.at[0], tok_s, sem_i)
        cp.start()
        cp.wait()

        def issue(blk, carry):
            r0 = pl.multiple_of(blk * ROW_DMA_BLOCK, ROW_DMA_BLOCK)
            dst = xg_ref.at[pl.ds(r0, ROW_DMA_BLOCK)]
            for r in range(ROW_DMA_BLOCK):
                pltpu.make_async_copy(x_hbm.at[pl.ds(tok_s[0, r0 + r], 1)], dst.at[pl.ds(r, 1)], sem_g).start(
                    priority=ROW_DMA_PRIORITY)
            return carry

        lax.fori_loop(0, tm // ROW_DMA_BLOCK, issue, 0)

    def wait_gather():
        pltpu.make_async_copy(x_hbm.at[pl.ds(0, tm)], xg_ref, sem_g).wait()

    def start_scatter():
        cp = pltpu.make_async_copy(dst_ref.at[0], dst_s, sem_i)
        cp.start()
        cp.wait()

        def issue(blk, carry):
            r0 = pl.multiple_of(blk * ROW_DMA_BLOCK, ROW_DMA_BLOCK)
            src = acc_ref.at[pl.ds(r0, ROW_DMA_BLOCK)]
            for r in range(ROW_DMA_BLOCK):
                pltpu.make_async_copy(src.at[pl.ds(r, 1)], y_hbm.at[pl.ds(dst_s[0, r0 + r], 1)], sem_s).start(
                    priority=ROW_DMA_PRIORITY)
            return carry

        lax.fori_loop(0, tm // ROW_DMA_BLOCK, issue, 0)

    def wait_scatter():
        pltpu.make_async_copy(acc_ref, y_hbm.at[pl.ds(0, tm)], sem_s).wait()

    @pl.when(active & (f == 0))
    def _():
        @pl.when(t == 0)
        def _():
            start_gather(tok_ref)

        wait_gather()
        hb_ref[...] = _rms(xg_ref[...], g_ref[...]).astype(BF16)

        @pl.when(next_active)
        def _():
            start_gather(tokn_ref)

        @pl.when(t > 0)
        def _():
            wait_scatter()

    @pl.when(active)
    def _():
        h = hb_ref[...]
        a = _dot(h, w1_ref[0])
        b = _dot(h, w3_ref[0])
        u = (a * _sigmoid(a) * b).astype(BF16)

        @pl.when(f == 0)
        def _():
            acc_ref[...] = _dot(u, w2_ref[0])

        @pl.when(f > 0)
        def _():
            acc_ref[...] += _dot(u, w2_ref[0])

    @pl.when(active & (f == last_f))
    def _():
        start_scatter()

        @pl.when(jnp.logical_not(next_active))
        def _():
            wait_scatter()


def _moe_experts(x, g, w1, w3, w2, tile_e, tile_nv, row_tok, row_dst, n_out_rows, tm, tf):
    n, d = x.shape
    dff = w1.shape[2]
    n_tiles = tile_e.shape[0]
    nf = dff // tf

    def wcol(t, f, te, nv):
        return (te[t], 0, jnp.where(nv[t] > 0, f, nf - 1))

    def wrow(t, f, te, nv):
        return (te[t], jnp.where(nv[t] > 0, f, nf - 1), 0)

    grid_spec = pltpu.PrefetchScalarGridSpec(
        num_scalar_prefetch=2,
        grid=(n_tiles, nf),
        in_specs=[
            pl.BlockSpec((1, 1, tm), lambda t, f, te, nv: (t, 0, 0)),
            pl.BlockSpec((1, 1, tm), lambda t, f, te, nv: (jnp.minimum(t + 1, n_tiles - 1), 0, 0)),
            pl.BlockSpec((1, 1, tm), lambda t, f, te, nv: (t, 0, 0)),
            pl.BlockSpec(memory_space=pl.ANY),
            pl.BlockSpec((1, d), lambda t, f, te, nv: (0, 0)),
            pl.BlockSpec((1, d, tf), wcol),
            pl.BlockSpec((1, d, tf), wcol),
            pl.BlockSpec((1, tf, d), wrow),
        ],
        out_specs=pl.BlockSpec(memory_space=pl.ANY),
        scratch_shapes=[
            pltpu.VMEM((tm, d), F32),
            pltpu.VMEM((tm, d), BF16),
            pltpu.VMEM((tm, d), F32),
            pltpu.SMEM((1, tm), jnp.int32),
            pltpu.SMEM((1, tm), jnp.int32),
            pltpu.SemaphoreType.DMA,
            pltpu.SemaphoreType.DMA,
            pltpu.SemaphoreType.DMA,
        ],
    )
    return pl.pallas_call(
        functools.partial(_moe_kernel, tm=tm),
        grid_spec=grid_spec,
        out_shape=jax.ShapeDtypeStruct((n_out_rows, d), F32),
        compiler_params=_params(("arbitrary", "arbitrary")),
        name="moe_experts",
    )(tile_e, tile_nv, row_tok, row_tok, row_dst, x, g, w1, w3, w2)


def _moe_plan(eidx, n_exp, tm, n_pad):
    n = eidx.shape[0]
    rows = n * TOP_K
    flat_e = eidx.reshape(-1)
    order = jnp.argsort(flat_e, stable=True).astype(jnp.int32)
    sizes = jnp.bincount(flat_e, length=n_exp).astype(jnp.int32)
    starts = jnp.cumsum(sizes) - sizes
    tiles_per = (sizes + tm - 1) // tm
    tile_end = jnp.cumsum(tiles_per)
    tile_start = tile_end - tiles_per
    n_tiles = -(-rows // tm) + n_exp
    t = jnp.arange(n_tiles, dtype=jnp.int32)
    te = jnp.minimum(jnp.searchsorted(tile_end, t, side="right"), n_exp - 1).astype(jnp.int32)
    off = (t - tile_start[te]) * tm
    nv = jnp.where(t < tile_end[-1], jnp.clip(sizes[te] - off, 0, tm), 0).astype(jnp.int32)
    r = jnp.arange(tm, dtype=jnp.int32)[None, :]
    valid = r < nv[:, None]
    flat = order[jnp.clip(starts[te][:, None] + off[:, None] + r, 0, rows - 1)]
    tok = flat // TOP_K
    slot = flat - tok * TOP_K
    row_tok = jnp.where(valid, tok, 0)
    row_dst = jnp.where(valid, slot * n_pad + tok, n + r)
    return te, nv, row_tok[:, None, :], row_dst[:, None, :]


def _combine_kernel(x_ref, y0_ref, y1_ref, gate_ref, g_ref, o_ref, *, final):
    gate = gate_ref[...]
    x = x_ref[...] + (y0_ref[0] * gate[:, 0:1] + y1_ref[0] * gate[:, 1:2])
    o_ref[...] = _rms(x, g_ref[...]) if final else x


def _combine(x, y2, gates, g, tm, final, row0, n_rows):
    d = x.shape[1]
    assert row0 % tm == 0 and n_rows % tm == 0
    i0 = row0 // tm
    return pl.pallas_call(
        functools.partial(_combine_kernel, final=final),
        grid=(n_rows // tm,),
        in_specs=[
            pl.BlockSpec((tm, d), lambda i: (i0 + i, 0)),
            pl.BlockSpec((1, tm, d), lambda i: (0, i0 + i, 0)),
            pl.BlockSpec((1, tm, d), lambda i: (1, i0 + i, 0)),
            pl.BlockSpec((tm, LANES), lambda i: (i0 + i, 0)),
            pl.BlockSpec((1, d), lambda i: (0, 0)),
        ],
        out_specs=pl.BlockSpec((tm, d), lambda i: (i, 0)),
        out_shape=jax.ShapeDtypeStruct((n_rows, d), F32),
        compiler_params=_params(("parallel",)),
        name="combine",
    )(x, y2, y2, gates, g)


def _norm_kernel(x_ref, g_ref, o_ref):
    o_ref[...] = _rms(x_ref[...], g_ref[...])


def _norm(x, g, tm, row0, n_rows):
    d = x.shape[1]
    assert row0 % tm == 0 and n_rows % tm == 0
    i0 = row0 // tm
    return pl.pallas_call(
        _norm_kernel,
        grid=(n_rows // tm,),
        in_specs=[pl.BlockSpec((tm, d), lambda i: (i0 + i, 0)), pl.BlockSpec((1, d), lambda i: (0, 0))],
        out_specs=pl.BlockSpec((tm, d), lambda i: (i, 0)),
        out_shape=jax.ShapeDtypeStruct((n_rows, d), F32),
        compiler_params=_params(("parallel",)),
        name="final_norm",
    )(x, g)


def kernel(x_prompt, x_sample, cache_conv, state_gla, meta_tokens, norm_mix_g, w_in, w_gate_a2, b_gate_a,
           conv_w, conv_b, conv_ln_g, conv_ln_b, w_conv_out, gla_norm_g, w_gla_out, w_out, norm_ffn_g,
           ffn_w1, ffn_w3, ffn_w2, router_w, exp_w1, exp_w3, exp_w2, final_norm_g):
    bp, seq, d = x_prompt.shape
    bs, ts, _ = x_sample.shape
    n_meta = meta_tokens.shape[0]
    depth = w_in.shape[0]
    c_dim = conv_w.shape[2]
    width = conv_w.shape[1]
    n_heads, dkh, dvh = state_gla.shape[2:]
    dk, dv = n_heads * dkh, n_heads * dvh
    rank = w_gate_a2.shape[1]
    n_exp = router_w.shape[2]
    assert rank <= LANES and n_exp <= LANES

    assert n_meta == ts, "meta and sample blocks share the short-sequence calls"
    n_pm, n_mt, n_s = bp * seq, bp * n_meta, bs * ts
    row_short, row_s = n_pm, n_pm + n_mt
    n = n_pm + n_mt + n_s
    meta = jnp.broadcast_to(meta_tokens[None].astype(x_prompt.dtype), (bp, n_meta, d))
    x = jnp.concatenate([x_prompt.reshape(n_pm, d), meta.reshape(n_mt, d), x_sample.reshape(n_s, d)], axis=0)

    in_q = 2 * c_dim
    in_alr = in_q + 2 * dk + 2 * dv
    col_q, col_k, col_v, col_g = 0, dk, 2 * dk, 2 * dk + dv
    col_zc = col_g + dv
    col_zg = col_zc + d
    n_p = col_zg + d

    tm_big = _pick_tile(n, 1280, BF16_ROWS)
    tm_mid = _pick_tile(n, 640, BF16_ROWS)
    tn_in = _pick_tile(math.gcd(n_p, 2 * c_dim), 1024, 2 * LANES)
    tm_in = _pick_tile(math.gcd(seq, n_mt + n_s), 512, BF16_ROWS)
    tn_d = _pick_tile(d, 512, LANES)
    tt_gla = _pick_tile(seq, 512, CHUNK)
    tm_out_p = _pick_tile(n_pm, 640, BF16_ROWS)
    tm_out_s = _pick_tile(math.gcd(row_s, n_s), 640, BF16_ROWS)
    row = lambda v: v.reshape(1, -1).astype(F32)

    new_conv_p, new_conv_s, new_gla_p, new_gla_s = [], [], [], []
    for i in range(depth):
        wi = w_in[i]
        half = tn_in // 2
        glu_cols = [wi[:, o + k * half:o + (k + 1) * half] for k in range(c_dim // half) for o in (0, c_dim)]
        wm = jnp.concatenate(glu_cols + [wi[:, in_q:in_alr], wi[:, in_alr + rank:]], axis=1).astype(BF16)
        wlr = jnp.pad(wi[:, in_alr:in_alr + rank], ((0, 0), (0, LANES - rank))).astype(BF16)
        wa2 = jnp.pad(w_gate_a2[i], ((0, LANES - rank), (0, 0))).astype(F32)
        w8 = jnp.broadcast_to(conv_w[i][:, None, :], (width, SUBLANES, c_dim)).astype(F32)
        cargs = (w8, row(conv_b[i]), row(conv_ln_g[i]), row(conv_ln_b[i]))
        p, lg, c_p, nc_p, glu_short = _inproj_conv(
            x, row(norm_mix_g[i]), wm, wlr, wa2, row(b_gate_a[i]), *cargs,
            tm=tm_in, tn=tn_in, c_dim=c_dim, n_pm=n_pm, seq=seq, n_meta=n_meta)
        hist0 = jnp.concatenate([jnp.zeros((bp,) + cache_conv.shape[2:], F32), cache_conv[i].astype(F32)], axis=0)
        c, nh_short = _conv_short(c_p, glu_short, hist0, *cargs, row0=row_short, tt=ts)
        nc_s = nh_short[bp:]

        gargs = dict(n=n, col_q=col_q, col_k=col_k, col_v=col_v, col_g=col_g, n_heads=n_heads, dkh=dkh, dvh=dvh)
        ng = row(gla_norm_g[i])
        s0 = jnp.concatenate([jnp.zeros((bp,) + state_gla.shape[2:], F32), state_gla[i].astype(F32)], axis=0)
        og_short, st_short = _gla_branch(p, lg, ng, s0, nb=bp + bs, t_len=ts, row0=row_short, tt=ts, **gargs)
        og, ns_p = _gla_branch(p, lg, ng, st_short, nb=bp, t_len=seq, row0=0, tt=tt_gla, prev=og_short, **gargs)
        ns_s = st_short[bp:]

        mixed = _mix(c, og, p, w_conv_out[i].astype(BF16), w_gla_out[i].astype(BF16), col_zc, col_zg,
                     tm_big, tn_d)
        x = _resid_mm(mixed, w_out[i].astype(BF16), x, tm_mid, d)

        j = i // 2
        last = i == depth - 1
        if i % 2 == 0:
            tf = _pick_tile(ffn_w1.shape[2], 512, LANES)
            x = _ffn(x, row(norm_ffn_g[i]), ffn_w1[j].astype(BF16), ffn_w3[j].astype(BF16),
                     ffn_w2[j].astype(BF16), tm_mid, tf)
            if last:
                y_p = _norm(x, row(final_norm_g), tm_out_p, 0, n_pm)
                y_s = _norm(x, row(final_norm_g), tm_out_s, row_s, n_s)
        else:
            rw = jnp.pad(router_w[j], ((0, 0), (0, LANES - n_exp))).astype(F32)
            eidx, gates = _router(x, row(norm_ffn_g[i]), rw, n_exp, tm_mid)
            tm_e = 1024 if n * TOP_K >= 8192 else 64
            n_pad = n + tm_e
            te, nv, row_tok, row_dst = _moe_plan(eidx[:, :TOP_K], n_exp, tm_e, n_pad)
            tf = _pick_tile(exp_w1.shape[3], 512, LANES)
            y2 = _moe_experts(x, row(norm_ffn_g[i]), exp_w1[j].astype(BF16), exp_w3[j].astype(BF16),
                              exp_w2[j].astype(BF16), te, nv, row_tok, row_dst, TOP_K * n_pad, tm_e, tf)
            y2 = y2.reshape(TOP_K, n_pad, d)
            if last:
                y_p = _combine(x, y2, gates, row(final_norm_g), tm_out_p, True, 0, n_pm)
                y_s = _combine(x, y2, gates, row(final_norm_g), tm_out_s, True, row_s, n_s)
            else:
                x = _combine(x, y2, gates, row(final_norm_g), tm_mid, False, 0, n)
        new_conv_p.append(nc_p)
        new_conv_s.append(nc_s)
        new_gla_p.append(ns_p)
        new_gla_s.append(ns_s)

    return (y_p.reshape(bp, seq, d), y_s.reshape(bs, ts, d), jnp.stack(new_conv_p), jnp.stack(new_gla_p),
            jnp.stack(new_conv_s), jnp.stack(new_gla_s))
```

```python
import functools
import math

import jax
import jax.numpy as jnp
from jax import lax
from jax.experimental import pallas as pl
from jax.experimental.pallas import tpu as pltpu

CHUNK = 64
SUB = 16
GATE_TEMP = 16.0
EPS = 1e-6
TOP_K = 2
LANES = 128
SUBLANES = 8
BF16_ROWS = 16
VMEM_LIMIT_BYTES = 56 * 1024 * 1024
ROW_DMA_PRIORITY = 1
ROW_DMA_BLOCK = 32
GLA_HEAD_GROUP = 4
MOE_SUB_FF = 512

F32 = jnp.float32
BF16 = jnp.bfloat16


def _pick_tile(n, target, mult):
    best = None
    for d in range(mult, min(n, target) + 1, mult):
        if n % d == 0:
            best = d
    assert best is not None, (n, target, mult)
    return best


def _params(sem):
    return pltpu.CompilerParams(dimension_semantics=sem, vmem_limit_bytes=VMEM_LIMIT_BYTES)


def _dot(a, b):
    return jnp.dot(a, b, preferred_element_type=F32)


def _split3(x):
    hi = x.astype(BF16)
    r1 = x - hi.astype(F32)
    mid = r1.astype(BF16)
    lo = (r1 - mid.astype(F32)).astype(BF16)
    return hi, mid, lo


def _dot_f32(a, b):
    a_hi, a_mid, _ = _split3(a)
    b_hi, b_mid, _ = _split3(b)
    return _dot(a_hi, b_hi) + _dot(a_mid, b_hi) + _dot(a_hi, b_mid)


def _rms(x, g):
    return x * lax.rsqrt(jnp.mean(x * x, axis=-1, keepdims=True) + EPS) * g


def _sigmoid(x):
    return jax.nn.sigmoid(x)


def _log_sigmoid(z):
    return jnp.minimum(z, 0.0) - jnp.log1p(jnp.exp(-jnp.abs(z)))


def _conv_taps(xs, w_ref, cols, off, width):
    return _conv_groups(xs, w_ref, cols, off, width, 1)[0]


def _conv_groups(xs, w_ref, cols, off, width, n_out):
    sub = lax.broadcasted_iota(jnp.int32, xs[0].shape, 0)
    totals = [None] * n_out
    for s in range(SUBLANES):
        taps = [(w, (off + w) // SUBLANES) for w in range(width) if (off + w) % SUBLANES == s]
        if not taps:
            continue
        shifted = {}
        for g in range(n_out):
            acc = None
            for w, a in taps:
                m = g + a
                if m not in shifted:
                    shifted[m] = xs[m] if s == 0 else jnp.where(sub >= s, xs[m], xs[m + 1])
                term = shifted[m] * w_ref[w, :, cols]
                acc = term if acc is None else acc + term
            if s:
                acc = pltpu.roll(acc, SUBLANES - s, 0)
            totals[g] = acc if totals[g] is None else totals[g] + acc
    return totals


def _ln_swish(y, cb, lng, lnb):
    y = y + cb
    mu = jnp.mean(y, axis=-1, keepdims=True)
    yc = y - mu
    c = yc * lax.rsqrt(jnp.mean(yc * yc, axis=-1, keepdims=True) + EPS) * lng + lnb
    return (c * _sigmoid(c)).astype(BF16)


def _inproj_conv_kernel(x_ref, g_ref, wm_ref, wlr_ref, wa2_ref, ba_ref, cw_ref, cb_ref, lng_ref, lnb_ref,
                        p_ref, lg_ref, c_ref, nh_ref, glu_hbm,
                        nb_ref, buf_ref, sg_ref, sem,
                        *, tm, n_glu, width, hp, n_short_tiles, tiles_per_seq, n_meta, units_per_step):
    i = pl.program_id(0)
    j = pl.program_id(1)
    hist = width - 1
    off = hp - hist
    half = buf_ref.shape[2]
    is_short = i < n_short_tiles
    ip = jnp.maximum(i - n_short_tiles, 0)
    seq_b = ip // tiles_per_seq
    tile_in_seq = ip - seq_b * tiles_per_seq
    n_groups = (off + width - 1) // SUBLANES + 2
    n_units = tm // BF16_ROWS
    n_conv_steps = -(-n_units // units_per_step)

    @pl.when(j == 0)
    def _():
        nb = _rms(x_ref[...], g_ref[...]).astype(BF16)
        nb_ref[...] = nb
        a_lr = _dot(nb, wlr_ref[...])
        z = _dot_f32(a_lr, wa2_ref[...]) + ba_ref[...]
        lg_ref[...] = _log_sigmoid(z) * (1.0 / GATE_TEMP)

        @pl.when(jnp.logical_and(jnp.logical_not(is_short), tile_in_seq == 0))
        def _():
            r0 = pl.multiple_of(seq_b * n_meta, SUBLANES)
            for k in range(n_glu):
                buf_ref[k, 0:hp, :] = jnp.zeros((hp, half), F32)
                buf_ref[k, hp - n_meta:hp, :] = sg_ref[k, pl.ds(r0, n_meta), :]

        @pl.when(jnp.logical_and(jnp.logical_not(is_short), tile_in_seq > 0))
        def _():
            for k in range(n_glu):
                buf_ref[k, 0:hp, :] = buf_ref[k, tm:tm + hp, :]

    @pl.when(j < n_glu)
    def _():
        r = _dot(nb_ref[...], wm_ref[...])
        glu = r[:, :half] * _sigmoid(r[:, half:])

        @pl.when(is_short)
        def _():
            sg_ref[j, pl.ds(pl.multiple_of(i * tm, SUBLANES), tm), :] = glu

        @pl.when(jnp.logical_not(is_short))
        def _():
            buf_ref[j, hp:hp + tm, :] = glu

    @pl.when(jnp.logical_and(j == n_glu - 1, is_short))
    def _():
        rows = pl.ds(pl.multiple_of(i * tm, SUBLANES), tm)
        cp = pltpu.make_async_copy(sg_ref.at[:, rows], glu_hbm.at[:, rows], sem)
        cp.start()
        cp.wait()

    @pl.when(jnp.logical_and(j == n_glu, jnp.logical_and(jnp.logical_not(is_short),
                                                        tile_in_seq == tiles_per_seq - 1)))
    def _():
        for k in range(n_glu):
            nh_ref[0, :, k * half:(k + 1) * half] = buf_ref[k, tm + off:tm + hp, :]

    do_conv = jnp.logical_and(jnp.logical_not(is_short), j - n_glu < n_conv_steps)

    @pl.when(jnp.logical_and(j >= n_glu, do_conv))
    def _():
        p_ref[...] = _dot(nb_ref[...], wm_ref[...])
        groups_per_unit = BF16_ROWS // SUBLANES
        n_out = units_per_step * groups_per_unit
        base = pl.multiple_of((j - n_glu) * (units_per_step * BF16_ROWS), BF16_ROWS)
        ys = []
        for k in range(n_glu):
            win = [buf_ref[k, pl.ds(base + SUBLANES * g, SUBLANES), :] for g in range(n_out + n_groups - 2)]
            ys.append(_conv_groups(win, cw_ref, slice(k * half, (k + 1) * half), off, width, n_out))
        for u in range(units_per_step):
            y = jnp.concatenate([jnp.concatenate([ys[k][u * groups_per_unit + g2] for k in range(n_glu)], axis=1)
                                 for g2 in range(groups_per_unit)], axis=0)
            c_ref[pl.ds(base + u * BF16_ROWS, BF16_ROWS), :] = _ln_swish(y, cb_ref[...], lng_ref[...], lnb_ref[...])

    @pl.when(jnp.logical_and(j >= n_glu, jnp.logical_not(do_conv)))
    def _():
        p_ref[...] = _dot(nb_ref[...], wm_ref[...])


def _inproj_conv(x, g, wm, wlr, wa2, ba, w8, cb, lng, lnb, *, tm, tn, c_dim, n_pm, seq, n_meta):
    n, d = x.shape
    half = tn // 2
    n_glu = c_dim // half
    n_p = wm.shape[1] - 2 * c_dim
    dk = wa2.shape[1]
    width = w8.shape[0]
    hp = -(-(width - 1) // SUBLANES) * SUBLANES
    n_short = n - n_pm
    assert c_dim % half == 0 and n_p % tn == 0 and n_pm % tm == 0 and n_short % tm == 0 and seq % tm == 0
    assert tm % BF16_ROWS == 0 and tm >= hp and n_meta <= width - 1 and n_meta % SUBLANES == 0
    ks, tps, bp = n_short // tm, seq // tm, n_pm // seq
    n_p_steps = n_p // tn
    units_per_step = -(-(tm // BF16_ROWS) // n_p_steps)
    assert (tm // BF16_ROWS) % units_per_step == 0 and tn % units_per_step == 0

    def rb(i):
        return jnp.where(i < ks, n_pm // tm + i, i - ks)

    vec = pl.BlockSpec((1, c_dim), lambda i, j: (0, 0))
    kern = functools.partial(_inproj_conv_kernel, tm=tm, n_glu=n_glu, width=width, hp=hp, n_short_tiles=ks,
                             tiles_per_seq=tps, n_meta=n_meta, units_per_step=units_per_step)
    return pl.pallas_call(
        kern,
        grid=(n // tm, n_glu + n_p_steps),
        in_specs=[
            pl.BlockSpec((tm, d), lambda i, j: (rb(i), 0)),
            pl.BlockSpec((1, d), lambda i, j: (0, 0)),
            pl.BlockSpec((d, tn), lambda i, j: (0, j)),
            pl.BlockSpec((d, LANES), lambda i, j: (0, 0)),
            pl.BlockSpec((LANES, dk), lambda i, j: (0, 0)),
            pl.BlockSpec((1, dk), lambda i, j: (0, 0)),
            pl.BlockSpec((width, SUBLANES, c_dim), lambda i, j: (0, 0, 0)), vec, vec, vec,
        ],
        out_specs=[
            pl.BlockSpec((tm, tn), lambda i, j: (rb(i), jnp.maximum(j - n_glu, 0))),
            pl.BlockSpec((tm, dk), lambda i, j: (rb(i), 0)),
            pl.BlockSpec((tm, c_dim), lambda i, j: (rb(i), 0)),
            pl.BlockSpec((1, width - 1, c_dim), lambda i, j: (jnp.clip((i - ks) // tps, 0, bp - 1), 0, 0)),
            pl.BlockSpec(memory_space=pl.ANY),
        ],
        out_shape=[jax.ShapeDtypeStruct((n, n_p), F32), jax.ShapeDtypeStruct((n, dk), F32),
                   jax.ShapeDtypeStruct((n, c_dim), BF16), jax.ShapeDtypeStruct((bp, width - 1, c_dim), F32),
                   jax.ShapeDtypeStruct((n_glu, n_short, half), F32)],
        scratch_shapes=[pltpu.VMEM((tm, d), BF16), pltpu.VMEM((n_glu, hp + tm, half), F32),
                        pltpu.VMEM((n_glu, n_short, half), F32), pltpu.SemaphoreType.DMA],
        compiler_params=_params(("arbitrary", "arbitrary")),
        name="inproj_conv",
    )(x, g, wm, wlr, wa2, ba, w8, cb, lng, lnb)


def _conv_short_kernel(c_prev, glu_ref, hist_ref, w_ref, cb_ref, lng_ref, lnb_ref, c_ref, nh_ref, buf_ref, *,
                       tt, width, hp):
    del c_prev
    hist = width - 1
    off = hp - hist
    n_glu, _, half = glu_ref.shape
    n_groups = (off + width - 1) // SUBLANES + 2
    for k in range(n_glu):
        buf_ref[k, 0:off, :] = jnp.zeros((off, half), F32)
        buf_ref[k, off:hp, :] = hist_ref[0, :, k * half:(k + 1) * half]
        buf_ref[k, hp:hp + tt, :] = glu_ref[k]
    ys = []
    for r0 in range(0, tt, SUBLANES):
        parts = []
        for k in range(n_glu):
            xs = [buf_ref[k, r0 + SUBLANES * g:r0 + SUBLANES * (g + 1), :] for g in range(n_groups - 1)]
            parts.append(_conv_taps(xs, w_ref, slice(k * half, (k + 1) * half), off, width))
        ys.append(jnp.concatenate(parts, axis=1))
    c_ref[...] = _ln_swish(jnp.concatenate(ys, axis=0), cb_ref[...], lng_ref[...], lnb_ref[...])
    for k in range(n_glu):
        nh_ref[0, :, k * half:(k + 1) * half] = buf_ref[k, tt + off:tt + hp, :]


def _conv_short(c_prev, glu3, hist, w8, cb, lng, lnb, *, row0, tt):
    n, c_dim = c_prev.shape
    n_glu, n_short, half = glu3.shape
    width = w8.shape[0]
    hp = -(-(width - 1) // SUBLANES) * SUBLANES
    nb = n_short // tt
    assert row0 % tt == 0 and tt % BF16_ROWS == 0 and n_short % tt == 0
    rb0 = row0 // tt
    vec = pl.BlockSpec((1, c_dim), lambda s: (0, 0))
    return pl.pallas_call(
        functools.partial(_conv_short_kernel, tt=tt, width=width, hp=hp),
        grid=(nb,),
        in_specs=[
            pl.BlockSpec(memory_space=pl.ANY),
            pl.BlockSpec((n_glu, tt, half), lambda s: (0, s, 0)),
            pl.BlockSpec((1, width - 1, c_dim), lambda s: (s, 0, 0)),
            pl.BlockSpec((width, SUBLANES, c_dim), lambda s: (0, 0, 0)), vec, vec, vec,
        ],
        out_specs=[
            pl.BlockSpec((tt, c_dim), lambda s: (rb0 + s, 0)),
            pl.BlockSpec((1, width - 1, c_dim), lambda s: (s, 0, 0)),
        ],
        out_shape=[jax.ShapeDtypeStruct((n, c_dim), BF16),
                   jax.ShapeDtypeStruct((nb, width - 1, c_dim), F32)],
        scratch_shapes=[pltpu.VMEM((n_glu, hp + tt, half), F32)],
        input_output_aliases={0: 0},
        compiler_params=_params(("parallel",)),
        name="conv_short",
    )(c_prev, glu3, hist, w8, cb, lng, lnb)


def _gla_block(q, k, v, lg, s_ref, c):
    n_heads, dkh, dvh = s_ref.shape
    dk = q.shape[1]
    row = lax.broadcasted_iota(jnp.int32, (c, c), 0)
    col = lax.broadcasted_iota(jnp.int32, (c, c), 1)
    tri = jnp.where(row >= col, 1.0, 0.0).astype(BF16)
    lg3 = _split3(lg)
    b = _dot(tri, lg3[0]) + _dot(tri, lg3[1]) + _dot(tri, lg3[2])
    b_last = b[c - 1:c, :]
    vb = v.astype(BF16)
    qe = (q * jnp.exp(b)).astype(BF16)
    kd = (k * jnp.exp(b_last - b)).astype(BF16)
    tdims = (((0,), (0,)), ((), ()))
    decay = jnp.exp(jnp.transpose(jnp.broadcast_to(b_last, (SUBLANES, dk)))[:, 0:1])

    qis, kis = [], []
    for i in range(c // SUB):
        r0 = i * SUB
        nk = r0 + SUB
        ref = b[r0 - 1:r0, :] if i > 0 else jnp.zeros((1, dk), F32)
        qis.append((q[r0:nk] * jnp.exp(b[r0:nk] - ref)).astype(BF16))
        kis.append((k[0:nk] * jnp.exp(ref - b[0:nk])).astype(BF16))

    kh = lambda x, h: x[:, h * dkh:(h + 1) * dkh]
    vh = lambda x, h: x[:, h * dvh:(h + 1) * dvh]
    heads = range(n_heads)
    o_state = [_dot(kh(qe, h), s_ref[h].astype(BF16)) for h in heads]
    kv = [lax.dot_general(kh(kd, h), vh(vb, h), tdims, preferred_element_type=F32) for h in heads]
    nt = (((1,), (1,)), ((), ()))
    a = [[lax.dot_general(kh(qis[i], h), kh(kis[i], h), nt, preferred_element_type=F32)
          for i in range(c // SUB)] for h in heads]
    outs = []
    for h in heads:
        parts = []
        for i in range(c // SUB):
            r0 = i * SUB
            nk = r0 + SUB
            causal = (lax.broadcasted_iota(jnp.int32, (SUB, nk), 0) + r0
                      >= lax.broadcasted_iota(jnp.int32, (SUB, nk), 1))
            parts.append(_dot(jnp.where(causal, a[h][i], 0.0).astype(BF16), vh(vb, h)[0:nk]))
        outs.append(o_state[h] + (jnp.concatenate(parts, axis=0) if len(parts) > 1 else parts[0]))
    for h in heads:
        s_ref[h] = s_ref[h] * decay[h * dkh:(h + 1) * dkh] + kv[h]
    return outs


def _gla_kernel(*refs, tt, c, n_heads, scale, has_prev):
    if has_prev:
        refs = refs[1:]
    q_ref, k_ref, v_ref, lg_ref, g_ref, ng_ref, s0_ref, o_ref, so_ref, s_ref = refs
    t = pl.program_id(1)
    dvh = s_ref.shape[2]

    @pl.when(t == 0)
    def _():
        s_ref[...] = s0_ref[0]

    dkh = s_ref.shape[1]
    hg = GLA_HEAD_GROUP if n_heads % GLA_HEAD_GROUP == 0 else n_heads

    def block(r0):
        rows = pl.ds(r0, c)
        for h0 in range(0, n_heads, hg):
            kc = slice(h0 * dkh, (h0 + hg) * dkh)
            outs = _gla_block(q_ref[rows, kc] * scale, k_ref[rows, kc], v_ref[rows, h0 * dvh:(h0 + hg) * dvh],
                              lg_ref[rows, kc], s_ref.at[h0:h0 + hg], c)
            for h in range(hg):
                vc = slice((h0 + h) * dvh, (h0 + h + 1) * dvh)
                g = g_ref[rows, vc]
                o_ref[rows, vc] = (_rms(outs[h], ng_ref[...]) * (g * _sigmoid(g))).astype(BF16)

    if tt == c:
        block(0)
    else:
        def body(j, carry):
            block(pl.multiple_of(j * c, c))
            return carry

        lax.fori_loop(0, tt // c, body, 0)

    @pl.when(t == pl.num_programs(1) - 1)
    def _():
        so_ref[0] = s_ref[...]


def _gla_branch(p, lg, ng, s0, *, n, nb, t_len, row0, tt, col_q, col_k, col_v, col_g, n_heads, dkh, dvh,
                prev=None):
    dk, dv = n_heads * dkh, n_heads * dvh
    c = min(CHUNK, t_len)
    assert t_len % tt == 0 and row0 % tt == 0 and tt % c == 0 and c % SUB == 0 and c % BF16_ROWS == 0
    nt = t_len // tt
    rb0 = row0 // tt
    assert col_q % dk == 0 and col_k % dk == 0 and col_v % dv == 0 and col_g % dv == 0
    cq, ck, cv, cg = col_q // dk, col_k // dk, col_v // dv, col_g // dv
    kern = functools.partial(_gla_kernel, tt=tt, c=c, n_heads=n_heads, scale=float(dkh) ** -0.5,
                             has_prev=prev is not None)
    in_specs = [pl.BlockSpec(memory_space=pl.ANY)] if prev is not None else []
    args = [prev] if prev is not None else []
    in_specs += [
        pl.BlockSpec((tt, dk), lambda b, t: (rb0 + b * nt + t, cq)),
        pl.BlockSpec((tt, dk), lambda b, t: (rb0 + b * nt + t, ck)),
        pl.BlockSpec((tt, dv), lambda b, t: (rb0 + b * nt + t, cv)),
        pl.BlockSpec((tt, dk), lambda b, t: (rb0 + b * nt + t, 0)),
        pl.BlockSpec((tt, dv), lambda b, t: (rb0 + b * nt + t, cg)),
        pl.BlockSpec((1, dvh), lambda b, t: (0, 0)),
        pl.BlockSpec((1, n_heads, dkh, dvh), lambda b, t: (b, 0, 0, 0)),
    ]
    args += [p, p, p, lg, p, ng, s0]
    return pl.pallas_call(
        kern,
        grid=(nb, nt),
        in_specs=in_specs,
        out_specs=[
            pl.BlockSpec((tt, dv), lambda b, t: (rb0 + b * nt + t, 0)),
            pl.BlockSpec((1, n_heads, dkh, dvh), lambda b, t: (b, 0, 0, 0)),
        ],
        out_shape=[jax.ShapeDtypeStruct((n, dv), BF16),
                   jax.ShapeDtypeStruct((nb, n_heads, dkh, dvh), F32)],
        scratch_shapes=[pltpu.VMEM((n_heads, dkh, dvh), F32)],
        input_output_aliases={0: 0} if prev is not None else {},
        compiler_params=_params(("parallel", "arbitrary")),
        name="gla_branch",
    )(*args)


def _mix_kernel(c_ref, og_ref, zc_ref, zg_ref, wc_ref, wg_ref, o_ref):
    a = _dot(c_ref[...], wc_ref[...])
    b = _dot(og_ref[...], wg_ref[...])
    o_ref[...] = (_sigmoid(zc_ref[...]) * a + _sigmoid(zg_ref[...]) * b).astype(BF16)


def _mix(c, og, p, wc, wg, col_zc, col_zg, tm, tn):
    n, d = c.shape
    d_out = wc.shape[1]
    assert col_zc % tn == 0 and col_zg % tn == 0
    jc, jg = col_zc // tn, col_zg // tn
    return pl.pallas_call(
        _mix_kernel,
        grid=(n // tm, d_out // tn),
        in_specs=[
            pl.BlockSpec((tm, d), lambda i, j: (i, 0)),
            pl.BlockSpec((tm, og.shape[1]), lambda i, j: (i, 0)),
            pl.BlockSpec((tm, tn), lambda i, j: (i, jc + j)),
            pl.BlockSpec((tm, tn), lambda i, j: (i, jg + j)),
            pl.BlockSpec((d, tn), lambda i, j: (0, j)),
            pl.BlockSpec((og.shape[1], tn), lambda i, j: (0, j)),
        ],
        out_specs=pl.BlockSpec((tm, tn), lambda i, j: (i, j)),
        out_shape=jax.ShapeDtypeStruct((n, d_out), BF16),
        compiler_params=_params(("parallel", "arbitrary")),
        name="mix",
    )(c, og, p, p, wc, wg)


def _resid_mm_kernel(a_ref, w_ref, x_ref, o_ref):
    o_ref[...] = x_ref[...] + _dot(a_ref[...], w_ref[...])


def _resid_mm(a, w, x, tm, tn):
    n, k = a.shape
    d_out = w.shape[1]
    return pl.pallas_call(
        _resid_mm_kernel,
        grid=(n // tm, d_out // tn),
        in_specs=[
            pl.BlockSpec((tm, k), lambda i, j: (i, 0)),
            pl.BlockSpec((k, tn), lambda i, j: (0, j)),
            pl.BlockSpec((tm, tn), lambda i, j: (i, j)),
        ],
        out_specs=pl.BlockSpec((tm, tn), lambda i, j: (i, j)),
        out_shape=jax.ShapeDtypeStruct((n, d_out), F32),
        compiler_params=_params(("parallel", "arbitrary")),
        name="out_proj",
    )(a, w, x)


def _ffn_kernel(x_ref, g_ref, w1_ref, w3_ref, w2_ref, o_ref, hb_ref):
    f = pl.program_id(1)

    @pl.when(f == 0)
    def _():
        x = x_ref[...]
        hb_ref[...] = _rms(x, g_ref[...]).astype(BF16)
        o_ref[...] = x

    h = hb_ref[...]
    a = _dot(h, w1_ref[...])
    b = _dot(h, w3_ref[...])
    t = (a * _sigmoid(a) * b).astype(BF16)
    o_ref[...] += _dot(t, w2_ref[...])


def _ffn(x, g, w1, w3, w2, tm, tf):
    n, d = x.shape
    dff = w1.shape[1]
    return pl.pallas_call(
        _ffn_kernel,
        grid=(n // tm, dff // tf),
        in_specs=[
            pl.BlockSpec((tm, d), lambda i, f: (i, 0)),
            pl.BlockSpec((1, d), lambda i, f: (0, 0)),
            pl.BlockSpec((d, tf), lambda i, f: (0, f)),
            pl.BlockSpec((d, tf), lambda i, f: (0, f)),
            pl.BlockSpec((tf, d), lambda i, f: (f, 0)),
        ],
        out_specs=pl.BlockSpec((tm, d), lambda i, f: (i, 0)),
        out_shape=jax.ShapeDtypeStruct((n, d), F32),
        scratch_shapes=[pltpu.VMEM((tm, d), BF16)],
        compiler_params=_params(("parallel", "arbitrary")),
        name="ffn",
    )(x, g, w1, w3, w2)


def _router_kernel(x_ref, g_ref, rw_ref, idx_ref, gate_ref, *, n_exp):
    h = _rms(x_ref[...], g_ref[...])
    logits = _dot_f32(h, rw_ref[...])
    lane = lax.broadcasted_iota(jnp.int32, logits.shape, 1).astype(F32)
    neg = jnp.float32(-jnp.inf)
    l1 = jnp.where(lane < n_exp, logits, neg)
    m1 = jnp.max(l1, axis=-1, keepdims=True)
    i1 = jnp.min(jnp.where(l1 == m1, lane, float(LANES)), axis=-1, keepdims=True)
    l2 = jnp.where(lane == i1, neg, l1)
    m2 = jnp.max(l2, axis=-1, keepdims=True)
    i2 = jnp.min(jnp.where(l2 == m2, lane, float(LANES)), axis=-1, keepdims=True)
    e = jnp.exp(m2 - m1)
    den = 1.0 + e
    idx_ref[...] = jnp.where(lane == 0, i1, jnp.where(lane == 1, i2, 0.0)).astype(jnp.int32)
    gate_ref[...] = jnp.where(lane == 0, 1.0 / den, jnp.where(lane == 1, e / den, 0.0))


def _router(x, g, rw, n_exp, tm):
    n, d = x.shape
    return pl.pallas_call(
        functools.partial(_router_kernel, n_exp=n_exp),
        grid=(n // tm,),
        in_specs=[
            pl.BlockSpec((tm, d), lambda i: (i, 0)),
            pl.BlockSpec((1, d), lambda i: (0, 0)),
            pl.BlockSpec((d, LANES), lambda i: (0, 0)),
        ],
        out_specs=[pl.BlockSpec((tm, LANES), lambda i: (i, 0)), pl.BlockSpec((tm, LANES), lambda i: (i, 0))],
        out_shape=[jax.ShapeDtypeStruct((n, LANES), jnp.int32), jax.ShapeDtypeStruct((n, LANES), F32)],
        compiler_params=_params(("parallel",)),
        name="router",
    )(x, g, rw)


def _moe_kernel(te_ref, nv_ref, tok_ref, tokn_ref, dst_ref, x_hbm, g_ref, w1_ref, w3_ref, w2_ref, y_hbm,
                xg_ref, hb_ref, acc_ref, tok_s, dst_s, sem_g, sem_s, sem_i, *, tm):
    t = pl.program_id(0)
    f = pl.program_id(1)
    n_tiles = pl.num_programs(0)
    last_f = pl.num_programs(1) - 1
    active = nv_ref[t] > 0
    next_active = jnp.logical_and(t + 1 < n_tiles, nv_ref[jnp.minimum(t + 1, n_tiles - 1)] > 0)

    def start_gather(idx_ref):
        cp = pltpu.make_async_copy(idx_ref.at[0], tok_s, sem_i)
        cp.start()
        cp.wait()

        def issue(blk, carry):
            r0 = pl.multiple_of(blk * ROW_DMA_BLOCK, ROW_DMA_BLOCK)
            dst = xg_ref.at[pl.ds(r0, ROW_DMA_BLOCK)]
            for r in range(ROW_DMA_BLOCK):
                pltpu.make_async_copy(x_hbm.at[pl.ds(tok_s[0, r0 + r], 1)], dst.at[pl.ds(r, 1)], sem_g).start(
                    priority=ROW_DMA_PRIORITY)
            return carry

        lax.fori_loop(0, tm // ROW_DMA_BLOCK, issue, 0)

    def wait_gather():
        pltpu.make_async_copy(x_hbm.at[pl.ds(0, tm)], xg_ref, sem_g).wait()

    def start_scatter():
        cp = pltpu.make_async_copy(dst_ref.at[0], dst_s, sem_i)
        cp.start()
        cp.wait()

        def issue(blk, carry):
            r0 = pl.multiple_of(blk * ROW_DMA_BLOCK, ROW_DMA_BLOCK)
            src = acc_ref.at[pl.ds(r0, ROW_DMA_BLOCK)]
            for r in range(ROW_DMA_BLOCK):
                pltpu.make_async_copy(src.at[pl.ds(r, 1)], y_hbm.at[pl.ds(dst_s[0, r0 + r], 1)], sem_s).start(
                    priority=ROW_DMA_PRIORITY)
            return carry

        lax.fori_loop(0, tm // ROW_DMA_BLOCK, issue, 0)

    def wait_scatter():
        pltpu.make_async_copy(acc_ref, y_hbm.at[pl.ds(0, tm)], sem_s).wait()

    @pl.when(active & (f == 0))
    def _():
        @pl.when(t == 0)
        def _():
            start_gather(tok_ref)

        wait_gather()
        hb_ref[...] = _rms(xg_ref[...], g_ref[...]).astype(BF16)

        @pl.when(next_active)
        def _():
            start_gather(tokn_ref)

        @pl.when(t > 0)
        def _():
            wait_scatter()

        acc_ref[...] = jnp.zeros(acc_ref.shape, F32)

    @pl.when(active)
    def _():
        h = hb_ref[...]
        tf = w1_ref.shape[2]
        sub = MOE_SUB_FF if tf % MOE_SUB_FF == 0 else tf
        for c0 in range(0, tf, sub):
            a = _dot(h, w1_ref[0, :, c0:c0 + sub])
            b = _dot(h, w3_ref[0, :, c0:c0 + sub])
            u = (a * _sigmoid(a) * b).astype(BF16)
            acc_ref[...] += _dot(u, w2_ref[0, c0:c0 + sub, :])

    @pl.when(active & (f == last_f))
    def _():
        start_scatter()

        @pl.when(jnp.logical_not(next_active))
        def _():
            wait_scatter()


def _moe_experts(x, g, w1, w3, w2, tile_e, tile_nv, row_tok, row_dst, n_out_rows, tm, tf):
    n, d = x.shape
    dff = w1.shape[2]
    n_tiles = tile_e.shape[0]
    nf = dff // tf
    assert tm % ROW_DMA_BLOCK == 0

    def wcol(t, f, te, nv):
        return (te[t], 0, jnp.where(nv[t] > 0, f, nf - 1))

    def wrow(t, f, te, nv):
        return (te[t], jnp.where(nv[t] > 0, f, nf - 1), 0)

    grid_spec = pltpu.PrefetchScalarGridSpec(
        num_scalar_prefetch=2,
        grid=(n_tiles, nf),
        in_specs=[
            pl.BlockSpec((1, 1, tm), lambda t, f, te, nv: (t, 0, 0)),
            pl.BlockSpec((1, 1, tm), lambda t, f, te, nv: (jnp.minimum(t + 1, n_tiles - 1), 0, 0)),
            pl.BlockSpec((1, 1, tm), lambda t, f, te, nv: (t, 0, 0)),
            pl.BlockSpec(memory_space=pl.ANY),
            pl.BlockSpec((1, d), lambda t, f, te, nv: (0, 0)),
            pl.BlockSpec((1, d, tf), wcol),
            pl.BlockSpec((1, d, tf), wcol),
            pl.BlockSpec((1, tf, d), wrow),
        ],
        out_specs=pl.BlockSpec(memory_space=pl.ANY),
        scratch_shapes=[
            pltpu.VMEM((tm, d), F32),
            pltpu.VMEM((tm, d), BF16),
            pltpu.VMEM((tm, d), F32),
            pltpu.SMEM((1, tm), jnp.int32),
            pltpu.SMEM((1, tm), jnp.int32),
            pltpu.SemaphoreType.DMA,
            pltpu.SemaphoreType.DMA,
            pltpu.SemaphoreType.DMA,
        ],
    )
    return pl.pallas_call(
        functools.partial(_moe_kernel, tm=tm),
        grid_spec=grid_spec,
        out_shape=jax.ShapeDtypeStruct((n_out_rows, d), F32),
        compiler_params=_params(("arbitrary", "arbitrary")),
        name="moe_experts",
    )(tile_e, tile_nv, row_tok, row_tok, row_dst, x, g, w1, w3, w2)


def _moe_plan(eidx, n_exp, tm, n_pad):
    n = eidx.shape[0]
    rows = n * TOP_K
    flat_e = eidx.reshape(-1)
    order = jnp.argsort(flat_e, stable=True).astype(jnp.int32)
    sizes = jnp.bincount(flat_e, length=n_exp).astype(jnp.int32)
    starts = jnp.cumsum(sizes) - sizes
    tiles_per = (sizes + tm - 1) // tm
    tile_end = jnp.cumsum(tiles_per)
    tile_start = tile_end - tiles_per
    n_tiles = -(-rows // tm) + n_exp
    t = jnp.arange(n_tiles, dtype=jnp.int32)
    te = jnp.minimum(jnp.searchsorted(tile_end, t, side="right"), n_exp - 1).astype(jnp.int32)
    off = (t - tile_start[te]) * tm
    nv = jnp.where(t < tile_end[-1], jnp.clip(sizes[te] - off, 0, tm), 0).astype(jnp.int32)
    r = jnp.arange(tm, dtype=jnp.int32)[None, :]
    valid = r < nv[:, None]
    flat = order[jnp.clip(starts[te][:, None] + off[:, None] + r, 0, rows - 1)]
    tok = flat // TOP_K
    slot = flat - tok * TOP_K
    row_tok = jnp.where(valid, tok, 0)
    row_dst = jnp.where(valid, slot * n_pad + tok, n + r)
    return te, nv, row_tok[:, None, :], row_dst[:, None, :]


def _combine_kernel(x_ref, y0_ref, y1_ref, gate_ref, g_ref, o_ref, *, final):
    gate = gate_ref[...]
    x = x_ref[...] + (y0_ref[0] * gate[:, 0:1] + y1_ref[0] * gate[:, 1:2])
    o_ref[...] = _rms(x, g_ref[...]) if final else x


def _combine(x, y2, gates, g, tm, final, row0, n_rows):
    d = x.shape[1]
    assert row0 % tm == 0 and n_rows % tm == 0
    i0 = row0 // tm
    return pl.pallas_call(
        functools.partial(_combine_kernel, final=final),
        grid=(n_rows // tm,),
        in_specs=[
            pl.BlockSpec((tm, d), lambda i: (i0 + i, 0)),
            pl.BlockSpec((1, tm, d), lambda i: (0, i0 + i, 0)),
            pl.BlockSpec((1, tm, d), lambda i: (1, i0 + i, 0)),
            pl.BlockSpec((tm, LANES), lambda i: (i0 + i, 0)),
            pl.BlockSpec((1, d), lambda i: (0, 0)),
        ],
        out_specs=pl.BlockSpec((tm, d), lambda i: (i, 0)),
        out_shape=jax.ShapeDtypeStruct((n_rows, d), F32),
        compiler_params=_params(("parallel",)),
        name="combine",
    )(x, y2, y2, gates, g)


def _norm_kernel(x_ref, g_ref, o_ref):
    o_ref[...] = _rms(x_ref[...], g_ref[...])


def _norm(x, g, tm, row0, n_rows):
    d = x.shape[1]
    assert row0 % tm == 0 and n_rows % tm == 0
    i0 = row0 // tm
    return pl.pallas_call(
        _norm_kernel,
        grid=(n_rows // tm,),
        in_specs=[pl.BlockSpec((tm, d), lambda i: (i0 + i, 0)), pl.BlockSpec((1, d), lambda i: (0, 0))],
        out_specs=pl.BlockSpec((tm, d), lambda i: (i, 0)),
        out_shape=jax.ShapeDtypeStruct((n_rows, d), F32),
        compiler_params=_params(("parallel",)),
        name="final_norm",
    )(x, g)


def kernel(x_prompt, x_sample, cache_conv, state_gla, meta_tokens, norm_mix_g, w_in, w_gate_a2, b_gate_a,
           conv_w, conv_b, conv_ln_g, conv_ln_b, w_conv_out, gla_norm_g, w_gla_out, w_out, norm_ffn_g,
           ffn_w1, ffn_w3, ffn_w2, router_w, exp_w1, exp_w3, exp_w2, final_norm_g):
    bp, seq, d = x_prompt.shape
    bs, ts, _ = x_sample.shape
    n_meta = meta_tokens.shape[0]
    depth = w_in.shape[0]
    c_dim = conv_w.shape[2]
    width = conv_w.shape[1]
    n_heads, dkh, dvh = state_gla.shape[2:]
    dk, dv = n_heads * dkh, n_heads * dvh
    rank = w_gate_a2.shape[1]
    n_exp = router_w.shape[2]
    assert rank <= LANES and n_exp <= LANES

    assert n_meta == ts, "meta and sample blocks share the short-sequence calls"
    n_pm, n_mt, n_s = bp * seq, bp * n_meta, bs * ts
    row_short, row_s = n_pm, n_pm + n_mt
    n = n_pm + n_mt + n_s
    meta = jnp.broadcast_to(meta_tokens[None].astype(x_prompt.dtype), (bp, n_meta, d))
    x = jnp.concatenate([x_prompt.reshape(n_pm, d), meta.reshape(n_mt, d), x_sample.reshape(n_s, d)], axis=0)

    in_q = 2 * c_dim
    in_alr = in_q + 2 * dk + 2 * dv
    col_q, col_k, col_v, col_g = 0, dk, 2 * dk, 2 * dk + dv
    col_zc = col_g + dv
    col_zg = col_zc + d
    n_p = col_zg + d

    tm_big = _pick_tile(n, 1280, BF16_ROWS)
    tm_mid = _pick_tile(n, 640, BF16_ROWS)
    tn_in = _pick_tile(math.gcd(n_p, 2 * c_dim), 1024, 2 * LANES)
    tm_in = _pick_tile(math.gcd(seq, n_mt + n_s), 512, BF16_ROWS)
    tn_d = _pick_tile(d, 512, LANES)
    tt_gla = _pick_tile(seq, 512, CHUNK)
    tm_out_p = _pick_tile(n_pm, 640, BF16_ROWS)
    tm_out_s = _pick_tile(math.gcd(row_s, n_s), 640, BF16_ROWS)
    row = lambda v: v.reshape(1, -1).astype(F32)

    new_conv_p, new_conv_s, new_gla_p, new_gla_s = [], [], [], []
    for i in range(depth):
        wi = w_in[i]
        half = tn_in // 2
        glu_cols = [wi[:, o + k * half:o + (k + 1) * half] for k in range(c_dim // half) for o in (0, c_dim)]
        wm = jnp.concatenate(glu_cols + [wi[:, in_q:in_alr], wi[:, in_alr + rank:]], axis=1).astype(BF16)
        wlr = jnp.pad(wi[:, in_alr:in_alr + rank], ((0, 0), (0, LANES - rank))).astype(BF16)
        wa2 = jnp.pad(w_gate_a2[i], ((0, LANES - rank), (0, 0))).astype(F32)
        w8 = jnp.broadcast_to(conv_w[i][:, None, :], (width, SUBLANES, c_dim)).astype(F32)
        cargs = (w8, row(conv_b[i]), row(conv_ln_g[i]), row(conv_ln_b[i]))
        p, lg, c_p, nc_p, glu_short = _inproj_conv(
            x, row(norm_mix_g[i]), wm, wlr, wa2, row(b_gate_a[i]), *cargs,
            tm=tm_in, tn=tn_in, c_dim=c_dim, n_pm=n_pm, seq=seq, n_meta=n_meta)
        hist0 = jnp.concatenate([jnp.zeros((bp,) + cache_conv.shape[2:], F32), cache_conv[i].astype(F32)], axis=0)
        c, nh_short = _conv_short(c_p, glu_short, hist0, *cargs, row0=row_short, tt=ts)
        nc_s = nh_short[bp:]

        gargs = dict(n=n, col_q=col_q, col_k=col_k, col_v=col_v, col_g=col_g, n_heads=n_heads, dkh=dkh, dvh=dvh)
        ng = row(gla_norm_g[i])
        s0 = jnp.concatenate([jnp.zeros((bp,) + state_gla.shape[2:], F32), state_gla[i].astype(F32)], axis=0)
        og_short, st_short = _gla_branch(p, lg, ng, s0, nb=bp + bs, t_len=ts, row0=row_short, tt=ts, **gargs)
        og, ns_p = _gla_branch(p, lg, ng, st_short, nb=bp, t_len=seq, row0=0, tt=tt_gla, prev=og_short, **gargs)
        ns_s = st_short[bp:]

        mixed = _mix(c, og, p, w_conv_out[i].astype(BF16), w_gla_out[i].astype(BF16), col_zc, col_zg,
                     tm_big, tn_d)
        x = _resid_mm(mixed, w_out[i].astype(BF16), x, tm_mid, d)

        j = i // 2
        last = i == depth - 1
        if i % 2 == 0:
            tf = _pick_tile(ffn_w1.shape[2], 512, LANES)
            x = _ffn(x, row(norm_ffn_g[i]), ffn_w1[j].astype(BF16), ffn_w3[j].astype(BF16),
                     ffn_w2[j].astype(BF16), tm_mid, tf)
            if last:
                y_p = _norm(x, row(final_norm_g), tm_out_p, 0, n_pm)
                y_s = _norm(x, row(final_norm_g), tm_out_s, row_s, n_s)
        else:
            rw = jnp.pad(router_w[j], ((0, 0), (0, LANES - n_exp))).astype(F32)
            eidx, gates = _router(x, row(norm_ffn_g[i]), rw, n_exp, tm_mid)
            tm_e = 1024 if n * TOP_K >= 8192 else 64
            n_pad = n + tm_e
            te, nv, row_tok, row_dst = _moe_plan(eidx[:, :TOP_K], n_exp, tm_e, n_pad)
            tf = _pick_tile(exp_w1.shape[3], 1024, LANES)
            y2 = _moe_experts(x, row(norm_ffn_g[i]), exp_w1[j].astype(BF16), exp_w3[j].astype(BF16),
                              exp_w2[j].astype(BF16), te, nv, row_tok, row_dst, TOP_K * n_pad, tm_e, tf)
            y2 = y2.reshape(TOP_K, n_pad, d)
            if last:
                y_p = _combine(x, y2, gates, row(final_norm_g), tm_out_p, True, 0, n_pm)
                y_s = _combine(x, y2, gates, row(final_norm_g), tm_out_s, True, row_s, n_s)
            else:
                x = _combine(x, y2, gates, row(final_norm_g), tm_mid, False, 0, n)
        new_conv_p.append(nc_p)
        new_conv_s.append(nc_s)
        new_gla_p.append(ns_p)
        new_gla_s.append(ns_s)

    return (y_p.reshape(bp, seq, d), y_s.reshape(bs, ts, d), jnp.stack(new_conv_p), jnp.stack(new_gla_p),
            jnp.stack(new_conv_s), jnp.stack(new_gla_s))
```

```python
import functools
import math

import jax
import jax.numpy as jnp
from jax import lax
from jax.experimental import pallas as pl
from jax.experimental.pallas import tpu as pltpu

CHUNK = 128
SUB = 16
GATE_TEMP = 16.0
EPS = 1e-6
TOP_K = 2
LANES = 128
SUBLANES = 8
BF16_ROWS = 16
VMEM_LIMIT_BYTES = 56 * 1024 * 1024
ROW_DMA_PRIORITY = 1
ROW_DMA_BLOCK = 32
GLA_HEAD_GROUP = 4
MOE_SUB_FF = 512

F32 = jnp.float32
BF16 = jnp.bfloat16


def _pick_tile(n, target, mult):
    best = None
    for d in range(mult, min(n, target) + 1, mult):
        if n % d == 0:
            best = d
    assert best is not None, (n, target, mult)
    return best


def _params(sem):
    return pltpu.CompilerParams(dimension_semantics=sem, vmem_limit_bytes=VMEM_LIMIT_BYTES)


def _dot(a, b):
    return jnp.dot(a, b, preferred_element_type=F32)


def _split3(x):
    hi = x.astype(BF16)
    r1 = x - hi.astype(F32)
    mid = r1.astype(BF16)
    lo = (r1 - mid.astype(F32)).astype(BF16)
    return hi, mid, lo


def _dot_f32(a, b):
    a_hi, a_mid, _ = _split3(a)
    b_hi, b_mid, _ = _split3(b)
    return _dot(a_hi, b_hi) + _dot(a_mid, b_hi) + _dot(a_hi, b_mid)


def _rms(x, g):
    return x * lax.rsqrt(jnp.mean(x * x, axis=-1, keepdims=True) + EPS) * g


def _sigmoid(x):
    return jax.nn.sigmoid(x)


def _log_sigmoid(z):
    return jnp.minimum(z, 0.0) - jnp.log1p(jnp.exp(-jnp.abs(z)))


def _conv_taps(xs, w_ref, cols, off, width):
    return _conv_groups(xs, w_ref, cols, off, width, 1)[0]


def _conv_groups(xs, w_ref, cols, off, width, n_out):
    sub = lax.broadcasted_iota(jnp.int32, xs[0].shape, 0)
    totals = [None] * n_out
    for s in range(SUBLANES):
        taps = [(w, (off + w) // SUBLANES) for w in range(width) if (off + w) % SUBLANES == s]
        if not taps:
            continue
        shifted = {}
        for g in range(n_out):
            acc = None
            for w, a in taps:
                m = g + a
                if m not in shifted:
                    shifted[m] = xs[m] if s == 0 else jnp.where(sub >= s, xs[m], xs[m + 1])
                term = shifted[m] * w_ref[w, :, cols]
                acc = term if acc is None else acc + term
            if s:
                acc = pltpu.roll(acc, SUBLANES - s, 0)
            totals[g] = acc if totals[g] is None else totals[g] + acc
    return totals


def _ln_swish(y, cb, lng, lnb):
    y = y + cb
    mu = jnp.mean(y, axis=-1, keepdims=True)
    yc = y - mu
    c = yc * lax.rsqrt(jnp.mean(yc * yc, axis=-1, keepdims=True) + EPS) * lng + lnb
    return (c * _sigmoid(c)).astype(BF16)


def _inproj_conv_kernel(x_ref, g_ref, wm_ref, wlr_ref, wa2_ref, ba_ref, cw_ref, cb_ref, lng_ref, lnb_ref,
                        p_ref, lg_ref, c_ref, nh_ref, glu_hbm,
                        nb_ref, buf_ref, sg_ref, sem,
                        *, tm, n_glu, width, hp, n_short_tiles, tiles_per_seq, n_meta, units_per_step):
    i = pl.program_id(0)
    j = pl.program_id(1)
    hist = width - 1
    off = hp - hist
    half = buf_ref.shape[2]
    is_short = i < n_short_tiles
    ip = jnp.maximum(i - n_short_tiles, 0)
    seq_b = ip // tiles_per_seq
    tile_in_seq = ip - seq_b * tiles_per_seq
    n_groups = (off + width - 1) // SUBLANES + 2
    n_units = tm // BF16_ROWS
    n_conv_steps = -(-n_units // units_per_step)

    @pl.when(j == 0)
    def _():
        nb = _rms(x_ref[...], g_ref[...]).astype(BF16)
        nb_ref[...] = nb
        a_lr = _dot(nb, wlr_ref[...])
        z = _dot_f32(a_lr, wa2_ref[...]) + ba_ref[...]
        lg_ref[...] = _log_sigmoid(z) * (1.0 / GATE_TEMP)

        @pl.when(jnp.logical_and(jnp.logical_not(is_short), tile_in_seq == 0))
        def _():
            r0 = pl.multiple_of(seq_b * n_meta, SUBLANES)
            for k in range(n_glu):
                buf_ref[k, 0:hp, :] = jnp.zeros((hp, half), F32)
                buf_ref[k, hp - n_meta:hp, :] = sg_ref[k, pl.ds(r0, n_meta), :]

        @pl.when(jnp.logical_and(jnp.logical_not(is_short), tile_in_seq > 0))
        def _():
            for k in range(n_glu):
                buf_ref[k, 0:hp, :] = buf_ref[k, tm:tm + hp, :]

    @pl.when(j < n_glu)
    def _():
        r = _dot(nb_ref[...], wm_ref[...])
        glu = r[:, :half] * _sigmoid(r[:, half:])

        @pl.when(is_short)
        def _():
            sg_ref[j, pl.ds(pl.multiple_of(i * tm, SUBLANES), tm), :] = glu

        @pl.when(jnp.logical_not(is_short))
        def _():
            buf_ref[j, hp:hp + tm, :] = glu

    @pl.when(jnp.logical_and(j == n_glu - 1, is_short))
    def _():
        rows = pl.ds(pl.multiple_of(i * tm, SUBLANES), tm)
        cp = pltpu.make_async_copy(sg_ref.at[:, rows], glu_hbm.at[:, rows], sem)
        cp.start()
        cp.wait()

    @pl.when(jnp.logical_and(j == n_glu, jnp.logical_and(jnp.logical_not(is_short),
                                                        tile_in_seq == tiles_per_seq - 1)))
    def _():
        for k in range(n_glu):
            nh_ref[0, :, k * half:(k + 1) * half] = buf_ref[k, tm + off:tm + hp, :]

    do_conv = jnp.logical_and(jnp.logical_not(is_short), j - n_glu < n_conv_steps)

    @pl.when(jnp.logical_and(j >= n_glu, do_conv))
    def _():
        p_ref[...] = _dot(nb_ref[...], wm_ref[...])
        groups_per_unit = BF16_ROWS // SUBLANES
        n_out = units_per_step * groups_per_unit
        base = pl.multiple_of((j - n_glu) * (units_per_step * BF16_ROWS), BF16_ROWS)
        ys = []
        for k in range(n_glu):
            win = [buf_ref[k, pl.ds(base + SUBLANES * g, SUBLANES), :] for g in range(n_out + n_groups - 2)]
            ys.append(_conv_groups(win, cw_ref, slice(k * half, (k + 1) * half), off, width, n_out))
        for u in range(units_per_step):
            y = jnp.concatenate([jnp.concatenate([ys[k][u * groups_per_unit + g2] for k in range(n_glu)], axis=1)
                                 for g2 in range(groups_per_unit)], axis=0)
            c_ref[pl.ds(base + u * BF16_ROWS, BF16_ROWS), :] = _ln_swish(y, cb_ref[...], lng_ref[...], lnb_ref[...])

    @pl.when(jnp.logical_and(j >= n_glu, jnp.logical_not(do_conv)))
    def _():
        p_ref[...] = _dot(nb_ref[...], wm_ref[...])


def _inproj_conv(x, g, wm, wlr, wa2, ba, w8, cb, lng, lnb, *, tm, tn, c_dim, n_pm, seq, n_meta):
    n, d = x.shape
    half = tn // 2
    n_glu = c_dim // half
    n_p = wm.shape[1] - 2 * c_dim
    dk = wa2.shape[1]
    width = w8.shape[0]
    hp = -(-(width - 1) // SUBLANES) * SUBLANES
    n_short = n - n_pm
    assert c_dim % half == 0 and n_p % tn == 0 and n_pm % tm == 0 and n_short % tm == 0 and seq % tm == 0
    assert tm % BF16_ROWS == 0 and tm >= hp and n_meta <= width - 1 and n_meta % SUBLANES == 0
    ks, tps, bp = n_short // tm, seq // tm, n_pm // seq
    n_p_steps = n_p // tn
    units_per_step = -(-(tm // BF16_ROWS) // n_p_steps)
    assert (tm // BF16_ROWS) % units_per_step == 0 and tn % units_per_step == 0

    def rb(i):
        return jnp.where(i < ks, n_pm // tm + i, i - ks)

    vec = pl.BlockSpec((1, c_dim), lambda i, j: (0, 0))
    kern = functools.partial(_inproj_conv_kernel, tm=tm, n_glu=n_glu, width=width, hp=hp, n_short_tiles=ks,
                             tiles_per_seq=tps, n_meta=n_meta, units_per_step=units_per_step)
    return pl.pallas_call(
        kern,
        grid=(n // tm, n_glu + n_p_steps),
        in_specs=[
            pl.BlockSpec((tm, d), lambda i, j: (rb(i), 0)),
            pl.BlockSpec((1, d), lambda i, j: (0, 0)),
            pl.BlockSpec((d, tn), lambda i, j: (0, j)),
            pl.BlockSpec((d, LANES), lambda i, j: (0, 0)),
            pl.BlockSpec((LANES, dk), lambda i, j: (0, 0)),
            pl.BlockSpec((1, dk), lambda i, j: (0, 0)),
            pl.BlockSpec((width, SUBLANES, c_dim), lambda i, j: (0, 0, 0)), vec, vec, vec,
        ],
        out_specs=[
            pl.BlockSpec((tm, tn), lambda i, j: (rb(i), jnp.maximum(j - n_glu, 0))),
            pl.BlockSpec((tm, dk), lambda i, j: (rb(i), 0)),
            pl.BlockSpec((tm, c_dim), lambda i, j: (rb(i), 0)),
            pl.BlockSpec((1, width - 1, c_dim), lambda i, j: (jnp.clip((i - ks) // tps, 0, bp - 1), 0, 0)),
            pl.BlockSpec(memory_space=pl.ANY),
        ],
        out_shape=[jax.ShapeDtypeStruct((n, n_p), F32), jax.ShapeDtypeStruct((n, dk), F32),
                   jax.ShapeDtypeStruct((n, c_dim), BF16), jax.ShapeDtypeStruct((bp, width - 1, c_dim), F32),
                   jax.ShapeDtypeStruct((n_glu, n_short, half), F32)],
        scratch_shapes=[pltpu.VMEM((tm, d), BF16), pltpu.VMEM((n_glu, hp + tm, half), F32),
                        pltpu.VMEM((n_glu, n_short, half), F32), pltpu.SemaphoreType.DMA],
        compiler_params=_params(("arbitrary", "arbitrary")),
        name="inproj_conv",
    )(x, g, wm, wlr, wa2, ba, w8, cb, lng, lnb)


def _conv_short_kernel(c_prev, glu_ref, hist_ref, w_ref, cb_ref, lng_ref, lnb_ref, c_ref, nh_ref, buf_ref, *,
                       tt, width, hp):
    del c_prev
    hist = width - 1
    off = hp - hist
    n_glu, _, half = glu_ref.shape
    n_groups = (off + width - 1) // SUBLANES + 2
    for k in range(n_glu):
        buf_ref[k, 0:off, :] = jnp.zeros((off, half), F32)
        buf_ref[k, off:hp, :] = hist_ref[0, :, k * half:(k + 1) * half]
        buf_ref[k, hp:hp + tt, :] = glu_ref[k]
    ys = []
    for r0 in range(0, tt, SUBLANES):
        parts = []
        for k in range(n_glu):
            xs = [buf_ref[k, r0 + SUBLANES * g:r0 + SUBLANES * (g + 1), :] for g in range(n_groups - 1)]
            parts.append(_conv_taps(xs, w_ref, slice(k * half, (k + 1) * half), off, width))
        ys.append(jnp.concatenate(parts, axis=1))
    c_ref[...] = _ln_swish(jnp.concatenate(ys, axis=0), cb_ref[...], lng_ref[...], lnb_ref[...])
    for k in range(n_glu):
        nh_ref[0, :, k * half:(k + 1) * half] = buf_ref[k, tt + off:tt + hp, :]


def _conv_short(c_prev, glu3, hist, w8, cb, lng, lnb, *, row0, tt):
    n, c_dim = c_prev.shape
    n_glu, n_short, half = glu3.shape
    width = w8.shape[0]
    hp = -(-(width - 1) // SUBLANES) * SUBLANES
    nb = n_short // tt
    assert row0 % tt == 0 and tt % BF16_ROWS == 0 and n_short % tt == 0
    rb0 = row0 // tt
    vec = pl.BlockSpec((1, c_dim), lambda s: (0, 0))
    return pl.pallas_call(
        functools.partial(_conv_short_kernel, tt=tt, width=width, hp=hp),
        grid=(nb,),
        in_specs=[
            pl.BlockSpec(memory_space=pl.ANY),
            pl.BlockSpec((n_glu, tt, half), lambda s: (0, s, 0)),
            pl.BlockSpec((1, width - 1, c_dim), lambda s: (s, 0, 0)),
            pl.BlockSpec((width, SUBLANES, c_dim), lambda s: (0, 0, 0)), vec, vec, vec,
        ],
        out_specs=[
            pl.BlockSpec((tt, c_dim), lambda s: (rb0 + s, 0)),
            pl.BlockSpec((1, width - 1, c_dim), lambda s: (s, 0, 0)),
        ],
        out_shape=[jax.ShapeDtypeStruct((n, c_dim), BF16),
                   jax.ShapeDtypeStruct((nb, width - 1, c_dim), F32)],
        scratch_shapes=[pltpu.VMEM((n_glu, hp + tt, half), F32)],
        input_output_aliases={0: 0},
        compiler_params=_params(("parallel",)),
        name="conv_short",
    )(c_prev, glu3, hist, w8, cb, lng, lnb)


def _gla_block(q, k, v, lg, s_ref, c):
    n_heads, dkh, dvh = s_ref.shape
    dk = q.shape[1]
    row = lax.broadcasted_iota(jnp.int32, (c, c), 0)
    col = lax.broadcasted_iota(jnp.int32, (c, c), 1)
    tri = jnp.where(row >= col, 1.0, 0.0).astype(BF16)
    lg3 = _split3(lg)
    b = _dot(tri, lg3[0]) + _dot(tri, lg3[1]) + _dot(tri, lg3[2])
    b_last = b[c - 1:c, :]
    vb = v.astype(BF16)
    qe = (q * jnp.exp(b)).astype(BF16)
    kd = (k * jnp.exp(b_last - b)).astype(BF16)
    tdims = (((0,), (0,)), ((), ()))
    decay = jnp.exp(jnp.transpose(jnp.broadcast_to(b_last, (SUBLANES, dk)))[:, 0:1])

    qis, kis = [], []
    for i in range(c // SUB):
        r0 = i * SUB
        nk = r0 + SUB
        ref = b[r0 - 1:r0, :] if i > 0 else jnp.zeros((1, dk), F32)
        qis.append((q[r0:nk] * jnp.exp(b[r0:nk] - ref)).astype(BF16))
        kis.append((k[0:nk] * jnp.exp(ref - b[0:nk])).astype(BF16))

    kh = lambda x, h: x[:, h * dkh:(h + 1) * dkh]
    vh = lambda x, h: x[:, h * dvh:(h + 1) * dvh]
    heads = range(n_heads)
    o_state = [_dot(kh(qe, h), s_ref[h].astype(BF16)) for h in heads]
    kv = [lax.dot_general(kh(kd, h), vh(vb, h), tdims, preferred_element_type=F32) for h in heads]
    nt = (((1,), (1,)), ((), ()))
    a = [[lax.dot_general(kh(qis[i], h), kh(kis[i], h), nt, preferred_element_type=F32)
          for i in range(c // SUB)] for h in heads]
    outs = []
    for h in heads:
        parts = []
        for i in range(c // SUB):
            r0 = i * SUB
            nk = r0 + SUB
            causal = (lax.broadcasted_iota(jnp.int32, (SUB, nk), 0) + r0
                      >= lax.broadcasted_iota(jnp.int32, (SUB, nk), 1))
            parts.append(_dot(jnp.where(causal, a[h][i], 0.0).astype(BF16), vh(vb, h)[0:nk]))
        outs.append(o_state[h] + (jnp.concatenate(parts, axis=0) if len(parts) > 1 else parts[0]))
    for h in heads:
        s_ref[h] = s_ref[h] * decay[h * dkh:(h + 1) * dkh] + kv[h]
    return outs


def _gla_kernel(*refs, tt, c, n_heads, scale, has_prev):
    if has_prev:
        refs = refs[1:]
    q_ref, k_ref, v_ref, lg_ref, g_ref, ng_ref, s0_ref, o_ref, so_ref, s_ref = refs
    t = pl.program_id(1)
    dvh = s_ref.shape[2]

    @pl.when(t == 0)
    def _():
        s_ref[...] = s0_ref[0]

    dkh = s_ref.shape[1]
    hg = GLA_HEAD_GROUP if n_heads % GLA_HEAD_GROUP == 0 else n_heads

    def block(r0):
        rows = pl.ds(r0, c)
        for h0 in range(0, n_heads, hg):
            kc = slice(h0 * dkh, (h0 + hg) * dkh)
            outs = _gla_block(q_ref[rows, kc] * scale, k_ref[rows, kc], v_ref[rows, h0 * dvh:(h0 + hg) * dvh],
                              lg_ref[rows, kc], s_ref.at[h0:h0 + hg], c)
            for h in range(hg):
                vc = slice((h0 + h) * dvh, (h0 + h + 1) * dvh)
                g = g_ref[rows, vc]
                o_ref[rows, vc] = (_rms(outs[h], ng_ref[...]) * (g * _sigmoid(g))).astype(BF16)

    if tt == c:
        block(0)
    else:
        def body(j, carry):
            block(pl.multiple_of(j * c, c))
            return carry

        lax.fori_loop(0, tt // c, body, 0)

    @pl.when(t == pl.num_programs(1) - 1)
    def _():
        so_ref[0] = s_ref[...]


def _gla_branch(p, lg, ng, s0, *, n, nb, t_len, row0, tt, col_q, col_k, col_v, col_g, n_heads, dkh, dvh,
                prev=None):
    dk, dv = n_heads * dkh, n_heads * dvh
    c = min(CHUNK, t_len)
    assert t_len % tt == 0 and row0 % tt == 0 and tt % c == 0 and c % SUB == 0 and c % BF16_ROWS == 0
    nt = t_len // tt
    rb0 = row0 // tt
    assert col_q % dk == 0 and col_k % dk == 0 and col_v % dv == 0 and col_g % dv == 0
    cq, ck, cv, cg = col_q // dk, col_k // dk, col_v // dv, col_g // dv
    kern = functools.partial(_gla_kernel, tt=tt, c=c, n_heads=n_heads, scale=float(dkh) ** -0.5,
                             has_prev=prev is not None)
    in_specs = [pl.BlockSpec(memory_space=pl.ANY)] if prev is not None else []
    args = [prev] if prev is not None else []
    in_specs += [
        pl.BlockSpec((tt, dk), lambda b, t: (rb0 + b * nt + t, cq)),
        pl.BlockSpec((tt, dk), lambda b, t: (rb0 + b * nt + t, ck)),
        pl.BlockSpec((tt, dv), lambda b, t: (rb0 + b * nt + t, cv)),
        pl.BlockSpec((tt, dk), lambda b, t: (rb0 + b * nt + t, 0)),
        pl.BlockSpec((tt, dv), lambda b, t: (rb0 + b * nt + t, cg)),
        pl.BlockSpec((1, dvh), lambda b, t: (0, 0)),
        pl.BlockSpec((1, n_heads, dkh, dvh), lambda b, t: (b, 0, 0, 0)),
    ]
    args += [p, p, p, lg, p, ng, s0]
    return pl.pallas_call(
        kern,
        grid=(nb, nt),
        in_specs=in_specs,
        out_specs=[
            pl.BlockSpec((tt, dv), lambda b, t: (rb0 + b * nt + t, 0)),
            pl.BlockSpec((1, n_heads, dkh, dvh), lambda b, t: (b, 0, 0, 0)),
        ],
        out_shape=[jax.ShapeDtypeStruct((n, dv), BF16),
                   jax.ShapeDtypeStruct((nb, n_heads, dkh, dvh), F32)],
        scratch_shapes=[pltpu.VMEM((n_heads, dkh, dvh), F32)],
        input_output_aliases={0: 0} if prev is not None else {},
        compiler_params=_params(("parallel", "arbitrary")),
        name="gla_branch",
    )(*args)


def _mix_kernel(c_ref, og_ref, zc_ref, zg_ref, wc_ref, wg_ref, o_ref):
    a = _dot(c_ref[...], wc_ref[...])
    b = _dot(og_ref[...], wg_ref[...])
    o_ref[...] = (_sigmoid(zc_ref[...]) * a + _sigmoid(zg_ref[...]) * b).astype(BF16)


def _mix(c, og, p, wc, wg, col_zc, col_zg, tm, tn):
    n, d = c.shape
    d_out = wc.shape[1]
    assert col_zc % tn == 0 and col_zg % tn == 0
    jc, jg = col_zc // tn, col_zg // tn
    return pl.pallas_call(
        _mix_kernel,
        grid=(n // tm, d_out // tn),
        in_specs=[
            pl.BlockSpec((tm, d), lambda i, j: (i, 0)),
            pl.BlockSpec((tm, og.shape[1]), lambda i, j: (i, 0)),
            pl.BlockSpec((tm, tn), lambda i, j: (i, jc + j)),
            pl.BlockSpec((tm, tn), lambda i, j: (i, jg + j)),
            pl.BlockSpec((d, tn), lambda i, j: (0, j)),
            pl.BlockSpec((og.shape[1], tn), lambda i, j: (0, j)),
        ],
        out_specs=pl.BlockSpec((tm, tn), lambda i, j: (i, j)),
        out_shape=jax.ShapeDtypeStruct((n, d_out), BF16),
        compiler_params=_params(("parallel", "arbitrary")),
        name="mix",
    )(c, og, p, p, wc, wg)


def _resid_mm_kernel(a_ref, w_ref, x_ref, o_ref):
    o_ref[...] = x_ref[...] + _dot(a_ref[...], w_ref[...])


def _resid_mm(a, w, x, tm, tn):
    n, k = a.shape
    d_out = w.shape[1]
    return pl.pallas_call(
        _resid_mm_kernel,
        grid=(n // tm, d_out // tn),
        in_specs=[
            pl.BlockSpec((tm, k), lambda i, j: (i, 0)),
            pl.BlockSpec((k, tn), lambda i, j: (0, j)),
            pl.BlockSpec((tm, tn), lambda i, j: (i, j)),
        ],
        out_specs=pl.BlockSpec((tm, tn), lambda i, j: (i, j)),
        out_shape=jax.ShapeDtypeStruct((n, d_out), F32),
        compiler_params=_params(("parallel", "arbitrary")),
        name="out_proj",
    )(a, w, x)


def _ffn_kernel(x_ref, g_ref, w1_ref, w3_ref, w2_ref, o_ref, hb_ref):
    f = pl.program_id(1)

    @pl.when(f == 0)
    def _():
        x = x_ref[...]
        hb_ref[...] = _rms(x, g_ref[...]).astype(BF16)
        o_ref[...] = x

    h = hb_ref[...]
    a = _dot(h, w1_ref[...])
    b = _dot(h, w3_ref[...])
    t = (a * _sigmoid(a) * b).astype(BF16)
    o_ref[...] += _dot(t, w2_ref[...])


def _ffn(x, g, w1, w3, w2, tm, tf):
    n, d = x.shape
    dff = w1.shape[1]
    return pl.pallas_call(
        _ffn_kernel,
        grid=(n // tm, dff // tf),
        in_specs=[
            pl.BlockSpec((tm, d), lambda i, f: (i, 0)),
            pl.BlockSpec((1, d), lambda i, f: (0, 0)),
            pl.BlockSpec((d, tf), lambda i, f: (0, f)),
            pl.BlockSpec((d, tf), lambda i, f: (0, f)),
            pl.BlockSpec((tf, d), lambda i, f: (f, 0)),
        ],
        out_specs=pl.BlockSpec((tm, d), lambda i, f: (i, 0)),
        out_shape=jax.ShapeDtypeStruct((n, d), F32),
        scratch_shapes=[pltpu.VMEM((tm, d), BF16)],
        compiler_params=_params(("parallel", "arbitrary")),
        name="ffn",
    )(x, g, w1, w3, w2)


def _router_kernel(x_ref, g_ref, rw_ref, idx_ref, gate_ref, *, n_exp):
    h = _rms(x_ref[...], g_ref[...])
    logits = _dot_f32(h, rw_ref[...])
    lane = lax.broadcasted_iota(jnp.int32, logits.shape, 1).astype(F32)
    neg = jnp.float32(-jnp.inf)
    l1 = jnp.where(lane < n_exp, logits, neg)
    m1 = jnp.max(l1, axis=-1, keepdims=True)
    i1 = jnp.min(jnp.where(l1 == m1, lane, float(LANES)), axis=-1, keepdims=True)
    l2 = jnp.where(lane == i1, neg, l1)
    m2 = jnp.max(l2, axis=-1, keepdims=True)
    i2 = jnp.min(jnp.where(l2 == m2, lane, float(LANES)), axis=-1, keepdims=True)
    e = jnp.exp(m2 - m1)
    den = 1.0 + e
    idx_ref[...] = jnp.where(lane == 0, i1, jnp.where(lane == 1, i2, 0.0)).astype(jnp.int32)
    gate_ref[...] = jnp.where(lane == 0, 1.0 / den, jnp.where(lane == 1, e / den, 0.0))


def _router(x, g, rw, n_exp, tm):
    n, d = x.shape
    return pl.pallas_call(
        functools.partial(_router_kernel, n_exp=n_exp),
        grid=(n // tm,),
        in_specs=[
            pl.BlockSpec((tm, d), lambda i: (i, 0)),
            pl.BlockSpec((1, d), lambda i: (0, 0)),
            pl.BlockSpec((d, LANES), lambda i: (0, 0)),
        ],
        out_specs=[pl.BlockSpec((tm, LANES), lambda i: (i, 0)), pl.BlockSpec((tm, LANES), lambda i: (i, 0))],
        out_shape=[jax.ShapeDtypeStruct((n, LANES), jnp.int32), jax.ShapeDtypeStruct((n, LANES), F32)],
        compiler_params=_params(("parallel",)),
        name="router",
    )(x, g, rw)


def _moe_kernel(te_ref, nv_ref, tok_ref, tokn_ref, dst_ref, x_hbm, g_ref, w1_ref, w3_ref, w2_ref, y_hbm,
                xg_ref, hb_ref, acc_ref, tok_s, dst_s, sem_g, sem_s, sem_i, *, tm):
    t = pl.program_id(0)
    f = pl.program_id(1)
    n_tiles = pl.num_programs(0)
    last_f = pl.num_programs(1) - 1
    active = nv_ref[t] > 0
    next_active = jnp.logical_and(t + 1 < n_tiles, nv_ref[jnp.minimum(t + 1, n_tiles - 1)] > 0)

    def start_gather(idx_ref):
        cp = pltpu.make_async_copy(idx_ref.at[0], tok_s, sem_i)
        cp.start()
        cp.wait()

        def issue(blk, carry):
            r0 = pl.multiple_of(blk * ROW_DMA_BLOCK, ROW_DMA_BLOCK)
            dst = xg_ref.at[pl.ds(r0, ROW_DMA_BLOCK)]
            for r in range(ROW_DMA_BLOCK):
                pltpu.make_async_copy(x_hbm.at[pl.ds(tok_s[0, r0 + r], 1)], dst.at[pl.ds(r, 1)], sem_g).start(
                    priority=ROW_DMA_PRIORITY)
            return carry

        lax.fori_loop(0, tm // ROW_DMA_BLOCK, issue, 0)

    def wait_gather():
        pltpu.make_async_copy(x_hbm.at[pl.ds(0, tm)], xg_ref, sem_g).wait()

    def start_scatter():
        cp = pltpu.make_async_copy(dst_ref.at[0], dst_s, sem_i)
        cp.start()
        cp.wait()

        def issue(blk, carry):
            r0 = pl.multiple_of(blk * ROW_DMA_BLOCK, ROW_DMA_BLOCK)
            src = acc_ref.at[pl.ds(r0, ROW_DMA_BLOCK)]
            for r in range(ROW_DMA_BLOCK):
                pltpu.make_async_copy(src.at[pl.ds(r, 1)], y_hbm.at[pl.ds(dst_s[0, r0 + r], 1)], sem_s).start(
                    priority=ROW_DMA_PRIORITY)
            return carry

        lax.fori_loop(0, tm // ROW_DMA_BLOCK, issue, 0)

    def wait_scatter():
        pltpu.make_async_copy(acc_ref, y_hbm.at[pl.ds(0, tm)], sem_s).wait()

    @pl.when(active & (f == 0))
    def _():
        @pl.when(t == 0)
        def _():
            start_gather(tok_ref)

        wait_gather()
        hb_ref[...] = _rms(xg_ref[...], g_ref[...]).astype(BF16)

        @pl.when(next_active)
        def _():
            start_gather(tokn_ref)

        @pl.when(t > 0)
        def _():
            wait_scatter()

        acc_ref[...] = jnp.zeros(acc_ref.shape, F32)

    @pl.when(active)
    def _():
        h = hb_ref[...]
        tf = w1_ref.shape[2]
        sub = MOE_SUB_FF if tf % MOE_SUB_FF == 0 else tf
        for c0 in range(0, tf, sub):
            a = _dot(h, w1_ref[0, :, c0:c0 + sub])
            b = _dot(h, w3_ref[0, :, c0:c0 + sub])
            u = (a * _sigmoid(a) * b).astype(BF16)
            acc_ref[...] += _dot(u, w2_ref[0, c0:c0 + sub, :])

    @pl.when(active & (f == last_f))
    def _():
        start_scatter()

        @pl.when(jnp.logical_not(next_active))
        def _():
            wait_scatter()


def _moe_experts(x, g, w1, w3, w2, tile_e, tile_nv, row_tok, row_dst, n_out_rows, tm, tf):
    n, d = x.shape
    dff = w1.shape[2]
    n_tiles = tile_e.shape[0]
    nf = dff // tf
    assert tm % ROW_DMA_BLOCK == 0

    def wcol(t, f, te, nv):
        return (te[t], 0, jnp.where(nv[t] > 0, f, nf - 1))

    def wrow(t, f, te, nv):
        return (te[t], jnp.where(nv[t] > 0, f, nf - 1), 0)

    grid_spec = pltpu.PrefetchScalarGridSpec(
        num_scalar_prefetch=2,
        grid=(n_tiles, nf),
        in_specs=[
            pl.BlockSpec((1, 1, tm), lambda t, f, te, nv: (t, 0, 0)),
            pl.BlockSpec((1, 1, tm), lambda t, f, te, nv: (jnp.minimum(t + 1, n_tiles - 1), 0, 0)),
            pl.BlockSpec((1, 1, tm), lambda t, f, te, nv: (t, 0, 0)),
            pl.BlockSpec(memory_space=pl.ANY),
            pl.BlockSpec((1, d), lambda t, f, te, nv: (0, 0)),
            pl.BlockSpec((1, d, tf), wcol),
            pl.BlockSpec((1, d, tf), wcol),
            pl.BlockSpec((1, tf, d), wrow),
        ],
        out_specs=pl.BlockSpec(memory_space=pl.ANY),
        scratch_shapes=[
            pltpu.VMEM((tm, d), F32),
            pltpu.VMEM((tm, d), BF16),
            pltpu.VMEM((tm, d), F32),
            pltpu.SMEM((1, tm), jnp.int32),
            pltpu.SMEM((1, tm), jnp.int32),
            pltpu.SemaphoreType.DMA,
            pltpu.SemaphoreType.DMA,
            pltpu.SemaphoreType.DMA,
        ],
    )
    return pl.pallas_call(
        functools.partial(_moe_kernel, tm=tm),
        grid_spec=grid_spec,
        out_shape=jax.ShapeDtypeStruct((n_out_rows, d), F32),
        compiler_params=_params(("arbitrary", "arbitrary")),
        name="moe_experts",
    )(tile_e, tile_nv, row_tok, row_tok, row_dst, x, g, w1, w3, w2)


def _moe_plan(eidx, n_exp, tm, n_pad):
    n = eidx.shape[0]
    rows = n * TOP_K
    flat_e = eidx.reshape(-1)
    order = jnp.argsort(flat_e, stable=True).astype(jnp.int32)
    sizes = jnp.bincount(flat_e, length=n_exp).astype(jnp.int32)
    starts = jnp.cumsum(sizes) - sizes
    tiles_per = (sizes + tm - 1) // tm
    tile_end = jnp.cumsum(tiles_per)
    tile_start = tile_end - tiles_per
    n_tiles = -(-rows // tm) + n_exp
    t = jnp.arange(n_tiles, dtype=jnp.int32)
    te = jnp.minimum(jnp.searchsorted(tile_end, t, side="right"), n_exp - 1).astype(jnp.int32)
    off = (t - tile_start[te]) * tm
    nv = jnp.where(t < tile_end[-1], jnp.clip(sizes[te] - off, 0, tm), 0).astype(jnp.int32)
    r = jnp.arange(tm, dtype=jnp.int32)[None, :]
    valid = r < nv[:, None]
    flat = order[jnp.clip(starts[te][:, None] + off[:, None] + r, 0, rows - 1)]
    tok = flat // TOP_K
    slot = flat - tok * TOP_K
    row_tok = jnp.where(valid, tok, 0)
    row_dst = jnp.where(valid, slot * n_pad + tok, n + r)
    return te, nv, row_tok[:, None, :], row_dst[:, None, :]


def _combine_kernel(x_ref, y0_ref, y1_ref, gate_ref, g_ref, o_ref, *, final):
    gate = gate_ref[...]
    x = x_ref[...] + (y0_ref[0] * gate[:, 0:1] + y1_ref[0] * gate[:, 1:2])
    o_ref[...] = _rms(x, g_ref[...]) if final else x


def _combine(x, y2, gates, g, tm, final, row0, n_rows):
    d = x.shape[1]
    assert row0 % tm == 0 and n_rows % tm == 0
    i0 = row0 // tm
    return pl.pallas_call(
        functools.partial(_combine_kernel, final=final),
        grid=(n_rows // tm,),
        in_specs=[
            pl.BlockSpec((tm, d), lambda i: (i0 + i, 0)),
            pl.BlockSpec((1, tm, d), lambda i: (0, i0 + i, 0)),
            pl.BlockSpec((1, tm, d), lambda i: (1, i0 + i, 0)),
            pl.BlockSpec((tm, LANES), lambda i: (i0 + i, 0)),
            pl.BlockSpec((1, d), lambda i: (0, 0)),
        ],
        out_specs=pl.BlockSpec((tm, d), lambda i: (i, 0)),
        out_shape=jax.ShapeDtypeStruct((n_rows, d), F32),
        compiler_params=_params(("parallel",)),
        name="combine",
    )(x, y2, y2, gates, g)


def _norm_kernel(x_ref, g_ref, o_ref):
    o_ref[...] = _rms(x_ref[...], g_ref[...])


def _norm(x, g, tm, row0, n_rows):
    d = x.shape[1]
    assert row0 % tm == 0 and n_rows % tm == 0
    i0 = row0 // tm
    return pl.pallas_call(
        _norm_kernel,
        grid=(n_rows // tm,),
        in_specs=[pl.BlockSpec((tm, d), lambda i: (i0 + i, 0)), pl.BlockSpec((1, d), lambda i: (0, 0))],
        out_specs=pl.BlockSpec((tm, d), lambda i: (i, 0)),
        out_shape=jax.ShapeDtypeStruct((n_rows, d), F32),
        compiler_params=_params(("parallel",)),
        name="final_norm",
    )(x, g)


def kernel(x_prompt, x_sample, cache_conv, state_gla, meta_tokens, norm_mix_g, w_in, w_gate_a2, b_gate_a,
           conv_w, conv_b, conv_ln_g, conv_ln_b, w_conv_out, gla_norm_g, w_gla_out, w_out, norm_ffn_g,
           ffn_w1, ffn_w3, ffn_w2, router_w, exp_w1, exp_w3, exp_w2, final_norm_g):
    bp, seq, d = x_prompt.shape
    bs, ts, _ = x_sample.shape
    n_meta = meta_tokens.shape[0]
    depth = w_in.shape[0]
    c_dim = conv_w.shape[2]
    width = conv_w.shape[1]
    n_heads, dkh, dvh = state_gla.shape[2:]
    dk, dv = n_heads * dkh, n_heads * dvh
    rank = w_gate_a2.shape[1]
    n_exp = router_w.shape[2]
    assert rank <= LANES and n_exp <= LANES

    assert n_meta == ts, "meta and sample blocks share the short-sequence calls"
    n_pm, n_mt, n_s = bp * seq, bp * n_meta, bs * ts
    row_short, row_s = n_pm, n_pm + n_mt
    n = n_pm + n_mt + n_s
    meta = jnp.broadcast_to(meta_tokens[None].astype(x_prompt.dtype), (bp, n_meta, d))
    x = jnp.concatenate([x_prompt.reshape(n_pm, d), meta.reshape(n_mt, d), x_sample.reshape(n_s, d)], axis=0)

    in_q = 2 * c_dim
    in_alr = in_q + 2 * dk + 2 * dv
    col_q, col_k, col_v, col_g = 0, dk, 2 * dk, 2 * dk + dv
    col_zc = col_g + dv
    col_zg = col_zc + d
    n_p = col_zg + d

    tm_big = _pick_tile(n, 1280, BF16_ROWS)
    tm_mid = _pick_tile(n, 640, BF16_ROWS)
    tn_in = _pick_tile(math.gcd(n_p, 2 * c_dim), 1024, 2 * LANES)
    tm_in = _pick_tile(math.gcd(seq, n_mt + n_s), 512, BF16_ROWS)
    tn_d = _pick_tile(d, 512, LANES)
    tt_gla = _pick_tile(seq, 512, CHUNK)
    tm_out_p = _pick_tile(n_pm, 640, BF16_ROWS)
    tm_out_s = _pick_tile(math.gcd(row_s, n_s), 640, BF16_ROWS)
    row = lambda v: v.reshape(1, -1).astype(F32)

    new_conv_p, new_conv_s, new_gla_p, new_gla_s = [], [], [], []
    for i in range(depth):
        wi = w_in[i]
        half = tn_in // 2
        glu_cols = [wi[:, o + k * half:o + (k + 1) * half] for k in range(c_dim // half) for o in (0, c_dim)]
        wm = jnp.concatenate(glu_cols + [wi[:, in_q:in_alr], wi[:, in_alr + rank:]], axis=1).astype(BF16)
        wlr = jnp.pad(wi[:, in_alr:in_alr + rank], ((0, 0), (0, LANES - rank))).astype(BF16)
        wa2 = jnp.pad(w_gate_a2[i], ((0, LANES - rank), (0, 0))).astype(F32)
        w8 = jnp.broadcast_to(conv_w[i][:, None, :], (width, SUBLANES, c_dim)).astype(F32)
        cargs = (w8, row(conv_b[i]), row(conv_ln_g[i]), row(conv_ln_b[i]))
        p, lg, c_p, nc_p, glu_short = _inproj_conv(
            x, row(norm_mix_g[i]), wm, wlr, wa2, row(b_gate_a[i]), *cargs,
            tm=tm_in, tn=tn_in, c_dim=c_dim, n_pm=n_pm, seq=seq, n_meta=n_meta)
        hist0 = jnp.concatenate([jnp.zeros((bp,) + cache_conv.shape[2:], F32), cache_conv[i].astype(F32)], axis=0)
        c, nh_short = _conv_short(c_p, glu_short, hist0, *cargs, row0=row_short, tt=ts)
        nc_s = nh_short[bp:]

        gargs = dict(n=n, col_q=col_q, col_k=col_k, col_v=col_v, col_g=col_g, n_heads=n_heads, dkh=dkh, dvh=dvh)
        ng = row(gla_norm_g[i])
        s0 = jnp.concatenate([jnp.zeros((bp,) + state_gla.shape[2:], F32), state_gla[i].astype(F32)], axis=0)
        og_short, st_short = _gla_branch(p, lg, ng, s0, nb=bp + bs, t_len=ts, row0=row_short, tt=ts, **gargs)
        og, ns_p = _gla_branch(p, lg, ng, st_short, nb=bp, t_len=seq, row0=0, tt=tt_gla, prev=og_short, **gargs)
        ns_s = st_short[bp:]

        mixed = _mix(c, og, p, w_conv_out[i].astype(BF16), w_gla_out[i].astype(BF16), col_zc, col_zg,
                     tm_big, tn_d)
        x = _resid_mm(mixed, w_out[i].astype(BF16), x, tm_mid, d)

        j = i // 2
        last = i == depth - 1
        if i % 2 == 0:
            tf = _pick_tile(ffn_w1.shape[2], 512, LANES)
            x = _ffn(x, row(norm_ffn_g[i]), ffn_w1[j].astype(BF16), ffn_w3[j].astype(BF16),
                     ffn_w2[j].astype(BF16), tm_mid, tf)
            if last:
                y_p = _norm(x, row(final_norm_g), tm_out_p, 0, n_pm)
                y_s = _norm(x, row(final_norm_g), tm_out_s, row_s, n_s)
        else:
            rw = jnp.pad(router_w[j], ((0, 0), (0, LANES - n_exp))).astype(F32)
            eidx, gates = _router(x, row(norm_ffn_g[i]), rw, n_exp, tm_mid)
            tm_e = 1024 if n * TOP_K >= 8192 else 64
            n_pad = n + tm_e
            te, nv, row_tok, row_dst = _moe_plan(eidx[:, :TOP_K], n_exp, tm_e, n_pad)
            tf = _pick_tile(exp_w1.shape[3], 1024, LANES)
            y2 = _moe_experts(x, row(norm_ffn_g[i]), exp_w1[j].astype(BF16), exp_w3[j].astype(BF16),
                              exp_w2[j].astype(BF16), te, nv, row_tok, row_dst, TOP_K * n_pad, tm_e, tf)
            y2 = y2.reshape(TOP_K, n_pad, d)
            if last:
                y_p = _combine(x, y2, gates, row(final_norm_g), tm_out_p, True, 0, n_pm)
                y_s = _combine(x, y2, gates, row(final_norm_g), tm_out_s, True, row_s, n_s)
            else:
                x = _combine(x, y2, gates, row(final_norm_g), tm_mid, False, 0, n)
        new_conv_p.append(nc_p)
        new_conv_s.append(nc_s)
        new_gla_p.append(ns_p)
        new_gla_s.append(ns_s)

    return (y_p.reshape(bp, seq, d), y_s.reshape(bs, ts, d), jnp.stack(new_conv_p), jnp.stack(new_gla_p),
            jnp.stack(new_conv_s), jnp.stack(new_gla_s))
```

```python
import functools
import math

import jax
import jax.numpy as jnp
from jax import lax
from jax.experimental import pallas as pl
from jax.experimental.pallas import tpu as pltpu

CHUNK = 128
SUB = 16
GATE_TEMP = 16.0
EPS = 1e-6
TOP_K = 2
LANES = 128
SUBLANES = 8
BF16_ROWS = 16
VMEM_LIMIT_BYTES = 56 * 1024 * 1024
ROW_DMA_PRIORITY = 1
ROW_DMA_BLOCK = 32
GLA_HEAD_GROUP = 4
MOE_SUB_FF = 512

F32 = jnp.float32
BF16 = jnp.bfloat16


def _pick_tile(n, target, mult):
    best = None
    for d in range(mult, min(n, target) + 1, mult):
        if n % d == 0:
            best = d
    assert best is not None, (n, target, mult)
    return best


def _params(sem):
    return pltpu.CompilerParams(dimension_semantics=sem, vmem_limit_bytes=VMEM_LIMIT_BYTES)


def _dot(a, b):
    return jnp.dot(a, b, preferred_element_type=F32)


def _split3(x):
    hi = x.astype(BF16)
    r1 = x - hi.astype(F32)
    mid = r1.astype(BF16)
    lo = (r1 - mid.astype(F32)).astype(BF16)
    return hi, mid, lo


def _dot_f32(a, b):
    a_hi, a_mid, _ = _split3(a)
    b_hi, b_mid, _ = _split3(b)
    return _dot(a_hi, b_hi) + _dot(a_mid, b_hi) + _dot(a_hi, b_mid)


def _rms(x, g):
    return x * lax.rsqrt(jnp.mean(x * x, axis=-1, keepdims=True) + EPS) * g


def _sigmoid(x):
    return jax.nn.sigmoid(x)


def _log_sigmoid(z):
    return jnp.minimum(z, 0.0) - jnp.log1p(jnp.exp(-jnp.abs(z)))


def _conv_taps(xs, w_ref, cols, off, width):
    return _conv_groups(xs, w_ref, cols, off, width, 1)[0]


def _conv_groups(xs, w_ref, cols, off, width, n_out):
    sub = lax.broadcasted_iota(jnp.int32, xs[0].shape, 0)
    totals = [None] * n_out
    for s in range(SUBLANES):
        taps = [(w, (off + w) // SUBLANES) for w in range(width) if (off + w) % SUBLANES == s]
        if not taps:
            continue
        shifted = {}
        for g in range(n_out):
            acc = None
            for w, a in taps:
                m = g + a
                if m not in shifted:
                    shifted[m] = xs[m] if s == 0 else jnp.where(sub >= s, xs[m], xs[m + 1])
                term = shifted[m] * w_ref[w, :, cols]
                acc = term if acc is None else acc + term
            if s:
                acc = pltpu.roll(acc, SUBLANES - s, 0)
            totals[g] = acc if totals[g] is None else totals[g] + acc
    return totals


def _ln_swish(y, cb, lng, lnb):
    y = y + cb
    mu = jnp.mean(y, axis=-1, keepdims=True)
    yc = y - mu
    c = yc * lax.rsqrt(jnp.mean(yc * yc, axis=-1, keepdims=True) + EPS) * lng + lnb
    return (c * _sigmoid(c)).astype(BF16)


def _inproj_conv_kernel(x_ref, xs_ref, g_ref, wm_ref, wlr_ref, wa2_ref, ba_ref, cw_ref, cb_ref, lng_ref, lnb_ref,
                        p_ref, lg_ref, c_ref, nh_ref, glu_hbm,
                        nb_ref, buf_ref, sg_ref, sem,
                        *, tm, n_glu, width, hp, n_short_tiles, tiles_per_seq, n_meta, units_per_step):
    i = pl.program_id(0)
    j = pl.program_id(1)
    hist = width - 1
    off = hp - hist
    half = buf_ref.shape[2]
    is_short = i < n_short_tiles
    ip = jnp.maximum(i - n_short_tiles, 0)
    seq_b = ip // tiles_per_seq
    tile_in_seq = ip - seq_b * tiles_per_seq
    n_groups = (off + width - 1) // SUBLANES + 2
    n_units = tm // BF16_ROWS
    n_conv_steps = -(-n_units // units_per_step)

    @pl.when(j == 0)
    def _():
        nb = _rms(jnp.where(is_short, xs_ref[...], x_ref[...]), g_ref[...]).astype(BF16)
        nb_ref[...] = nb
        a_lr = _dot(nb, wlr_ref[...])
        z = _dot_f32(a_lr, wa2_ref[...]) + ba_ref[...]
        lg_ref[...] = _log_sigmoid(z) * (1.0 / GATE_TEMP)

        @pl.when(jnp.logical_and(jnp.logical_not(is_short), tile_in_seq == 0))
        def _():
            r0 = pl.multiple_of(seq_b * n_meta, SUBLANES)
            for k in range(n_glu):
                buf_ref[k, 0:hp, :] = jnp.zeros((hp, half), F32)
                buf_ref[k, hp - n_meta:hp, :] = sg_ref[k, pl.ds(r0, n_meta), :]

        @pl.when(jnp.logical_and(jnp.logical_not(is_short), tile_in_seq > 0))
        def _():
            for k in range(n_glu):
                buf_ref[k, 0:hp, :] = buf_ref[k, tm:tm + hp, :]

    @pl.when(j < n_glu)
    def _():
        r = _dot(nb_ref[...], wm_ref[...])
        glu = r[:, :half] * _sigmoid(r[:, half:])

        @pl.when(is_short)
        def _():
            sg_ref[j, pl.ds(pl.multiple_of(i * tm, SUBLANES), tm), :] = glu

        @pl.when(jnp.logical_not(is_short))
        def _():
            buf_ref[j, hp:hp + tm, :] = glu

    @pl.when(jnp.logical_and(j == n_glu - 1, is_short))
    def _():
        rows = pl.ds(pl.multiple_of(i * tm, SUBLANES), tm)
        cp = pltpu.make_async_copy(sg_ref.at[:, rows], glu_hbm.at[:, rows], sem)
        cp.start()
        cp.wait()

    @pl.when(jnp.logical_and(j == n_glu, jnp.logical_and(jnp.logical_not(is_short),
                                                        tile_in_seq == tiles_per_seq - 1)))
    def _():
        for k in range(n_glu):
            nh_ref[0, :, k * half:(k + 1) * half] = buf_ref[k, tm + off:tm + hp, :]

    do_conv = jnp.logical_and(jnp.logical_not(is_short), j - n_glu < n_conv_steps)

    @pl.when(jnp.logical_and(j >= n_glu, do_conv))
    def _():
        p_ref[...] = _dot(nb_ref[...], wm_ref[...])
        groups_per_unit = BF16_ROWS // SUBLANES
        n_out = units_per_step * groups_per_unit
        base = pl.multiple_of((j - n_glu) * (units_per_step * BF16_ROWS), BF16_ROWS)
        ys = []
        for k in range(n_glu):
            win = [buf_ref[k, pl.ds(base + SUBLANES * g, SUBLANES), :] for g in range(n_out + n_groups - 2)]
            ys.append(_conv_groups(win, cw_ref, slice(k * half, (k + 1) * half), off, width, n_out))
        for u in range(units_per_step):
            y = jnp.concatenate([jnp.concatenate([ys[k][u * groups_per_unit + g2] for k in range(n_glu)], axis=1)
                                 for g2 in range(groups_per_unit)], axis=0)
            c_ref[pl.ds(base + u * BF16_ROWS, BF16_ROWS), :] = _ln_swish(y, cb_ref[...], lng_ref[...], lnb_ref[...])

    @pl.when(jnp.logical_and(j >= n_glu, jnp.logical_not(do_conv)))
    def _():
        p_ref[...] = _dot(nb_ref[...], wm_ref[...])


def _inproj_conv(x, xs, xs_row0, g, wm, wlr, wa2, ba, w8, cb, lng, lnb, *, tm, tn, c_dim, n_pm, n_short, seq,
                 n_meta):
    d = x.shape[1]
    n = n_pm + n_short
    half = tn // 2
    n_glu = c_dim // half
    n_p = wm.shape[1] - 2 * c_dim
    dk = wa2.shape[1]
    width = w8.shape[0]
    hp = -(-(width - 1) // SUBLANES) * SUBLANES
    assert xs_row0 % tm == 0 and x.shape[0] >= n_pm and xs.shape[0] >= xs_row0 + n_short
    assert c_dim % half == 0 and n_p % tn == 0 and n_pm % tm == 0 and n_short % tm == 0 and seq % tm == 0
    assert tm % BF16_ROWS == 0 and tm >= hp and n_meta <= width - 1 and n_meta % SUBLANES == 0
    ks, tps, bp = n_short // tm, seq // tm, n_pm // seq
    n_p_steps = n_p // tn
    units_per_step = -(-(tm // BF16_ROWS) // n_p_steps)
    assert (tm // BF16_ROWS) % units_per_step == 0 and tn % units_per_step == 0

    def rb(i):
        return jnp.where(i < ks, n_pm // tm + i, i - ks)

    vec = pl.BlockSpec((1, c_dim), lambda i, j: (0, 0))
    kern = functools.partial(_inproj_conv_kernel, tm=tm, n_glu=n_glu, width=width, hp=hp, n_short_tiles=ks,
                             tiles_per_seq=tps, n_meta=n_meta, units_per_step=units_per_step)
    return pl.pallas_call(
        kern,
        grid=(n // tm, n_glu + n_p_steps),
        in_specs=[
            pl.BlockSpec((tm, d), lambda i, j: (jnp.maximum(i - ks, 0), 0)),
            pl.BlockSpec((tm, d), lambda i, j: (xs_row0 // tm + jnp.minimum(i, ks - 1), 0)),
            pl.BlockSpec((1, d), lambda i, j: (0, 0)),
            pl.BlockSpec((d, tn), lambda i, j: (0, j)),
            pl.BlockSpec((d, LANES), lambda i, j: (0, 0)),
            pl.BlockSpec((LANES, dk), lambda i, j: (0, 0)),
            pl.BlockSpec((1, dk), lambda i, j: (0, 0)),
            pl.BlockSpec((width, SUBLANES, c_dim), lambda i, j: (0, 0, 0)), vec, vec, vec,
        ],
        out_specs=[
            pl.BlockSpec((tm, tn), lambda i, j: (rb(i), jnp.maximum(j - n_glu, 0))),
            pl.BlockSpec((tm, dk), lambda i, j: (rb(i), 0)),
            pl.BlockSpec((tm, c_dim), lambda i, j: (rb(i), 0)),
            pl.BlockSpec((1, width - 1, c_dim), lambda i, j: (jnp.clip((i - ks) // tps, 0, bp - 1), 0, 0)),
            pl.BlockSpec(memory_space=pl.ANY),
        ],
        out_shape=[jax.ShapeDtypeStruct((n, n_p), F32), jax.ShapeDtypeStruct((n, dk), F32),
                   jax.ShapeDtypeStruct((n, c_dim), BF16), jax.ShapeDtypeStruct((bp, width - 1, c_dim), F32),
                   jax.ShapeDtypeStruct((n_glu, n_short, half), F32)],
        scratch_shapes=[pltpu.VMEM((tm, d), BF16), pltpu.VMEM((n_glu, hp + tm, half), F32),
                        pltpu.VMEM((n_glu, n_short, half), F32), pltpu.SemaphoreType.DMA],
        compiler_params=_params(("arbitrary", "arbitrary")),
        name="inproj_conv",
    )(x, xs, g, wm, wlr, wa2, ba, w8, cb, lng, lnb)


def _conv_short_kernel(c_prev, glu_ref, hist_ref, w_ref, cb_ref, lng_ref, lnb_ref, c_ref, nh_ref, buf_ref, *,
                       tt, width, hp, n_zero):
    del c_prev
    hist = width - 1
    off = hp - hist
    n_glu, _, half = glu_ref.shape
    n_groups = (off + width - 1) // SUBLANES + 2
    for k in range(n_glu):
        buf_ref[k, 0:hp, :] = jnp.zeros((hp, half), F32)
        buf_ref[k, hp:hp + tt, :] = glu_ref[k]

    @pl.when(pl.program_id(0) >= n_zero)
    def _():
        for k in range(n_glu):
            buf_ref[k, off:hp, :] = hist_ref[0, :, k * half:(k + 1) * half]

    ys = []
    for r0 in range(0, tt, SUBLANES):
        parts = []
        for k in range(n_glu):
            xs = [buf_ref[k, r0 + SUBLANES * g:r0 + SUBLANES * (g + 1), :] for g in range(n_groups - 1)]
            parts.append(_conv_taps(xs, w_ref, slice(k * half, (k + 1) * half), off, width))
        ys.append(jnp.concatenate(parts, axis=1))
    c_ref[...] = _ln_swish(jnp.concatenate(ys, axis=0), cb_ref[...], lng_ref[...], lnb_ref[...])
    for k in range(n_glu):
        nh_ref[0, :, k * half:(k + 1) * half] = buf_ref[k, tt + off:tt + hp, :]


def _conv_short(c_prev, glu3, hist, w8, cb, lng, lnb, *, row0, tt):
    n, c_dim = c_prev.shape
    n_glu, n_short, half = glu3.shape
    width = w8.shape[0]
    hp = -(-(width - 1) // SUBLANES) * SUBLANES
    nb = n_short // tt
    n_zero = nb - hist.shape[0]
    assert row0 % tt == 0 and tt % BF16_ROWS == 0 and n_short % tt == 0 and n_zero >= 0
    rb0 = row0 // tt
    vec = pl.BlockSpec((1, c_dim), lambda s: (0, 0))
    return pl.pallas_call(
        functools.partial(_conv_short_kernel, tt=tt, width=width, hp=hp, n_zero=n_zero),
        grid=(nb,),
        in_specs=[
            pl.BlockSpec(memory_space=pl.ANY),
            pl.BlockSpec((n_glu, tt, half), lambda s: (0, s, 0)),
            pl.BlockSpec((1, width - 1, c_dim), lambda s: (jnp.maximum(s - n_zero, 0), 0, 0)),
            pl.BlockSpec((width, SUBLANES, c_dim), lambda s: (0, 0, 0)), vec, vec, vec,
        ],
        out_specs=[
            pl.BlockSpec((tt, c_dim), lambda s: (rb0 + s, 0)),
            pl.BlockSpec((1, width - 1, c_dim), lambda s: (s, 0, 0)),
        ],
        out_shape=[jax.ShapeDtypeStruct((n, c_dim), BF16),
                   jax.ShapeDtypeStruct((nb, width - 1, c_dim), F32)],
        scratch_shapes=[pltpu.VMEM((n_glu, hp + tt, half), F32)],
        input_output_aliases={0: 0},
        compiler_params=_params(("parallel",)),
        name="conv_short",
    )(c_prev, glu3, hist, w8, cb, lng, lnb)


def _gla_block(q, k, v, lg, s_ref, c):
    n_heads, dkh, dvh = s_ref.shape
    dk = q.shape[1]
    row = lax.broadcasted_iota(jnp.int32, (c, c), 0)
    col = lax.broadcasted_iota(jnp.int32, (c, c), 1)
    tri = jnp.where(row >= col, 1.0, 0.0).astype(BF16)
    lg3 = _split3(lg)
    b = _dot(tri, lg3[0]) + _dot(tri, lg3[1]) + _dot(tri, lg3[2])
    b_last = b[c - 1:c, :]
    vb = v.astype(BF16)
    qe = (q * jnp.exp(b)).astype(BF16)
    kd = (k * jnp.exp(b_last - b)).astype(BF16)
    tdims = (((0,), (0,)), ((), ()))
    decay = jnp.exp(jnp.transpose(jnp.broadcast_to(b_last, (SUBLANES, dk)))[:, 0:1])

    qis, kis = [], []
    for i in range(c // SUB):
        r0 = i * SUB
        nk = r0 + SUB
        ref = b[r0 - 1:r0, :] if i > 0 else jnp.zeros((1, dk), F32)
        qis.append((q[r0:nk] * jnp.exp(b[r0:nk] - ref)).astype(BF16))
        kis.append((k[0:nk] * jnp.exp(ref - b[0:nk])).astype(BF16))

    kh = lambda x, h: x[:, h * dkh:(h + 1) * dkh]
    vh = lambda x, h: x[:, h * dvh:(h + 1) * dvh]
    heads = range(n_heads)
    o_state = [_dot(kh(qe, h), s_ref[h].astype(BF16)) for h in heads]
    kv = [lax.dot_general(kh(kd, h), vh(vb, h), tdims, preferred_element_type=F32) for h in heads]
    nt = (((1,), (1,)), ((), ()))
    a = [[lax.dot_general(kh(qis[i], h), kh(kis[i], h), nt, preferred_element_type=F32)
          for i in range(c // SUB)] for h in heads]
    outs = []
    for h in heads:
        parts = []
        for i in range(c // SUB):
            r0 = i * SUB
            nk = r0 + SUB
            causal = (lax.broadcasted_iota(jnp.int32, (SUB, nk), 0) + r0
                      >= lax.broadcasted_iota(jnp.int32, (SUB, nk), 1))
            parts.append(_dot(jnp.where(causal, a[h][i], 0.0).astype(BF16), vh(vb, h)[0:nk]))
        outs.append(o_state[h] + (jnp.concatenate(parts, axis=0) if len(parts) > 1 else parts[0]))
    for h in heads:
        s_ref[h] = s_ref[h] * decay[h * dkh:(h + 1) * dkh] + kv[h]
    return outs


def _gla_kernel(*refs, tt, c, n_heads, scale, has_prev, n_zero):
    if has_prev:
        refs = refs[1:]
    q_ref, k_ref, v_ref, lg_ref, g_ref, ng_ref, s0_ref, o_ref, so_ref, s_ref = refs
    t = pl.program_id(1)
    dvh = s_ref.shape[2]

    @pl.when(t == 0)
    def _():
        s_ref[...] = s0_ref[0]

    @pl.when(jnp.logical_and(t == 0, pl.program_id(0) < n_zero))
    def _():
        s_ref[...] = jnp.zeros(s_ref.shape, F32)

    dkh = s_ref.shape[1]
    hg = GLA_HEAD_GROUP if n_heads % GLA_HEAD_GROUP == 0 else n_heads

    def block(r0):
        rows = pl.ds(r0, c)
        for h0 in range(0, n_heads, hg):
            kc = slice(h0 * dkh, (h0 + hg) * dkh)
            outs = _gla_block(q_ref[rows, kc] * scale, k_ref[rows, kc], v_ref[rows, h0 * dvh:(h0 + hg) * dvh],
                              lg_ref[rows, kc], s_ref.at[h0:h0 + hg], c)
            for h in range(hg):
                vc = slice((h0 + h) * dvh, (h0 + h + 1) * dvh)
                g = g_ref[rows, vc]
                o_ref[rows, vc] = (_rms(outs[h], ng_ref[...]) * (g * _sigmoid(g))).astype(BF16)

    if tt == c:
        block(0)
    else:
        def body(j, carry):
            block(pl.multiple_of(j * c, c))
            return carry

        lax.fori_loop(0, tt // c, body, 0)

    @pl.when(t == pl.num_programs(1) - 1)
    def _():
        so_ref[0] = s_ref[...]


def _gla_branch(p, lg, ng, s0, *, n, nb, t_len, row0, tt, col_q, col_k, col_v, col_g, n_heads, dkh, dvh,
                prev=None, n_zero=0):
    dk, dv = n_heads * dkh, n_heads * dvh
    c = min(CHUNK, t_len)
    assert t_len % tt == 0 and row0 % tt == 0 and tt % c == 0 and c % SUB == 0 and c % BF16_ROWS == 0
    nt = t_len // tt
    rb0 = row0 // tt
    assert col_q % dk == 0 and col_k % dk == 0 and col_v % dv == 0 and col_g % dv == 0
    cq, ck, cv, cg = col_q // dk, col_k // dk, col_v // dv, col_g // dv
    kern = functools.partial(_gla_kernel, tt=tt, c=c, n_heads=n_heads, scale=float(dkh) ** -0.5,
                             has_prev=prev is not None, n_zero=n_zero)
    in_specs = [pl.BlockSpec(memory_space=pl.ANY)] if prev is not None else []
    args = [prev] if prev is not None else []
    in_specs += [
        pl.BlockSpec((tt, dk), lambda b, t: (rb0 + b * nt + t, cq)),
        pl.BlockSpec((tt, dk), lambda b, t: (rb0 + b * nt + t, ck)),
        pl.BlockSpec((tt, dv), lambda b, t: (rb0 + b * nt + t, cv)),
        pl.BlockSpec((tt, dk), lambda b, t: (rb0 + b * nt + t, 0)),
        pl.BlockSpec((tt, dv), lambda b, t: (rb0 + b * nt + t, cg)),
        pl.BlockSpec((1, dvh), lambda b, t: (0, 0)),
        pl.BlockSpec((1, n_heads, dkh, dvh), lambda b, t: (jnp.maximum(b - n_zero, 0), 0, 0, 0)),
    ]
    args += [p, p, p, lg, p, ng, s0]
    return pl.pallas_call(
        kern,
        grid=(nb, nt),
        in_specs=in_specs,
        out_specs=[
            pl.BlockSpec((tt, dv), lambda b, t: (rb0 + b * nt + t, 0)),
            pl.BlockSpec((1, n_heads, dkh, dvh), lambda b, t: (b, 0, 0, 0)),
        ],
        out_shape=[jax.ShapeDtypeStruct((n, dv), BF16),
                   jax.ShapeDtypeStruct((nb, n_heads, dkh, dvh), F32)],
        scratch_shapes=[pltpu.VMEM((n_heads, dkh, dvh), F32)],
        input_output_aliases={0: 0} if prev is not None else {},
        compiler_params=_params(("parallel", "arbitrary")),
        name="gla_branch",
    )(*args)


def _mix_kernel(c_ref, og_ref, zc_ref, zg_ref, wc_ref, wg_ref, o_ref):
    a = _dot(c_ref[...], wc_ref[...])
    b = _dot(og_ref[...], wg_ref[...])
    o_ref[...] = (_sigmoid(zc_ref[...]) * a + _sigmoid(zg_ref[...]) * b).astype(BF16)


def _mix(c, og, p, wc, wg, col_zc, col_zg, tm, tn):
    n, d = c.shape
    d_out = wc.shape[1]
    assert col_zc % tn == 0 and col_zg % tn == 0
    jc, jg = col_zc // tn, col_zg // tn
    return pl.pallas_call(
        _mix_kernel,
        grid=(n // tm, d_out // tn),
        in_specs=[
            pl.BlockSpec((tm, d), lambda i, j: (i, 0)),
            pl.BlockSpec((tm, og.shape[1]), lambda i, j: (i, 0)),
            pl.BlockSpec((tm, tn), lambda i, j: (i, jc + j)),
            pl.BlockSpec((tm, tn), lambda i, j: (i, jg + j)),
            pl.BlockSpec((d, tn), lambda i, j: (0, j)),
            pl.BlockSpec((og.shape[1], tn), lambda i, j: (0, j)),
        ],
        out_specs=pl.BlockSpec((tm, tn), lambda i, j: (i, j)),
        out_shape=jax.ShapeDtypeStruct((n, d_out), BF16),
        compiler_params=_params(("parallel", "arbitrary")),
        name="mix",
    )(c, og, p, p, wc, wg)


def _resid_mm_kernel(a_ref, w_ref, x_ref, xs_ref, o_ref, *, n_first):
    x = jnp.where(pl.program_id(0) < n_first, x_ref[...], xs_ref[...])
    o_ref[...] = x + _dot(a_ref[...], w_ref[...])


def _resid_mm(a, w, x, xs, xs_row0, n_pm, tm, tn):
    n, k = a.shape
    d_out = w.shape[1]
    assert n_pm % tm == 0 and xs_row0 % tm == 0 and n % tm == 0
    n_first = n_pm // tm
    return pl.pallas_call(
        functools.partial(_resid_mm_kernel, n_first=n_first),
        grid=(n // tm, d_out // tn),
        in_specs=[
            pl.BlockSpec((tm, k), lambda i, j: (i, 0)),
            pl.BlockSpec((k, tn), lambda i, j: (0, j)),
            pl.BlockSpec((tm, tn), lambda i, j: (jnp.minimum(i, n_first - 1), j)),
            pl.BlockSpec((tm, tn), lambda i, j: (xs_row0 // tm + jnp.maximum(i - n_first, 0), j)),
        ],
        out_specs=pl.BlockSpec((tm, tn), lambda i, j: (i, j)),
        out_shape=jax.ShapeDtypeStruct((n, d_out), F32),
        compiler_params=_params(("parallel", "arbitrary")),
        name="out_proj",
    )(a, w, x, xs)


def _ffn_kernel(x_ref, g_ref, w1_ref, w3_ref, w2_ref, o_ref, hb_ref):
    f = pl.program_id(1)

    @pl.when(f == 0)
    def _():
        x = x_ref[...]
        hb_ref[...] = _rms(x, g_ref[...]).astype(BF16)
        o_ref[...] = x

    h = hb_ref[...]
    a = _dot(h, w1_ref[...])
    b = _dot(h, w3_ref[...])
    t = (a * _sigmoid(a) * b).astype(BF16)
    o_ref[...] += _dot(t, w2_ref[...])


def _ffn(x, g, w1, w3, w2, tm, tf):
    n, d = x.shape
    dff = w1.shape[1]
    return pl.pallas_call(
        _ffn_kernel,
        grid=(n // tm, dff // tf),
        in_specs=[
            pl.BlockSpec((tm, d), lambda i, f: (i, 0)),
            pl.BlockSpec((1, d), lambda i, f: (0, 0)),
            pl.BlockSpec((d, tf), lambda i, f: (0, f)),
            pl.BlockSpec((d, tf), lambda i, f: (0, f)),
            pl.BlockSpec((tf, d), lambda i, f: (f, 0)),
        ],
        out_specs=pl.BlockSpec((tm, d), lambda i, f: (i, 0)),
        out_shape=jax.ShapeDtypeStruct((n, d), F32),
        scratch_shapes=[pltpu.VMEM((tm, d), BF16)],
        compiler_params=_params(("parallel", "arbitrary")),
        name="ffn",
    )(x, g, w1, w3, w2)


def _router_kernel(x_ref, g_ref, rw_ref, idx_ref, gate_ref, *, n_exp):
    h = _rms(x_ref[...], g_ref[...])
    logits = _dot_f32(h, rw_ref[...])
    lane = lax.broadcasted_iota(jnp.int32, logits.shape, 1).astype(F32)
    neg = jnp.float32(-jnp.inf)
    l1 = jnp.where(lane < n_exp, logits, neg)
    m1 = jnp.max(l1, axis=-1, keepdims=True)
    i1 = jnp.min(jnp.where(l1 == m1, lane, float(LANES)), axis=-1, keepdims=True)
    l2 = jnp.where(lane == i1, neg, l1)
    m2 = jnp.max(l2, axis=-1, keepdims=True)
    i2 = jnp.min(jnp.where(l2 == m2, lane, float(LANES)), axis=-1, keepdims=True)
    e = jnp.exp(m2 - m1)
    den = 1.0 + e
    idx_ref[...] = jnp.where(lane == 0, i1, jnp.where(lane == 1, i2, 0.0)).astype(jnp.int32)
    gate_ref[...] = jnp.where(lane == 0, 1.0 / den, jnp.where(lane == 1, e / den, 0.0))


def _router(x, g, rw, n_exp, tm):
    n, d = x.shape
    return pl.pallas_call(
        functools.partial(_router_kernel, n_exp=n_exp),
        grid=(n // tm,),
        in_specs=[
            pl.BlockSpec((tm, d), lambda i: (i, 0)),
            pl.BlockSpec((1, d), lambda i: (0, 0)),
            pl.BlockSpec((d, LANES), lambda i: (0, 0)),
        ],
        out_specs=[pl.BlockSpec((tm, LANES), lambda i: (i, 0)), pl.BlockSpec((tm, LANES), lambda i: (i, 0))],
        out_shape=[jax.ShapeDtypeStruct((n, LANES), jnp.int32), jax.ShapeDtypeStruct((n, LANES), F32)],
        compiler_params=_params(("parallel",)),
        name="router",
    )(x, g, rw)


def _moe_kernel(te_ref, nv_ref, tok_ref, tokn_ref, dst_ref, x_hbm, g_ref, w1_ref, w3_ref, w2_ref, y_hbm,
                xg_ref, hb_ref, acc_ref, tok_s, dst_s, sem_g, sem_s, sem_i, *, tm):
    t = pl.program_id(0)
    f = pl.program_id(1)
    n_tiles = pl.num_programs(0)
    last_f = pl.num_programs(1) - 1
    active = nv_ref[t] > 0
    next_active = jnp.logical_and(t + 1 < n_tiles, nv_ref[jnp.minimum(t + 1, n_tiles - 1)] > 0)

    def start_gather(idx_ref):
        cp = pltpu.make_async_copy(idx_ref.at[0], tok_s, sem_i)
        cp.start()
        cp.wait()

        def issue(blk, carry):
            r0 = pl.multiple_of(blk * ROW_DMA_BLOCK, ROW_DMA_BLOCK)
            dst = xg_ref.at[pl.ds(r0, ROW_DMA_BLOCK)]
            for r in range(ROW_DMA_BLOCK):
                pltpu.make_async_copy(x_hbm.at[pl.ds(tok_s[0, r0 + r], 1)], dst.at[pl.ds(r, 1)], sem_g).start(
                    priority=ROW_DMA_PRIORITY)
            return carry

        lax.fori_loop(0, tm // ROW_DMA_BLOCK, issue, 0)

    def wait_gather():
        pltpu.make_async_copy(x_hbm.at[pl.ds(0, tm)], xg_ref, sem_g).wait()

    def start_scatter():
        cp = pltpu.make_async_copy(dst_ref.at[0], dst_s, sem_i)
        cp.start()
        cp.wait()

        def issue(blk, carry):
            r0 = pl.multiple_of(blk * ROW_DMA_BLOCK, ROW_DMA_BLOCK)
            src = acc_ref.at[pl.ds(r0, ROW_DMA_BLOCK)]
            for r in range(ROW_DMA_BLOCK):
                pltpu.make_async_copy(src.at[pl.ds(r, 1)], y_hbm.at[pl.ds(dst_s[0, r0 + r], 1)], sem_s).start(
                    priority=ROW_DMA_PRIORITY)
            return carry

        lax.fori_loop(0, tm // ROW_DMA_BLOCK, issue, 0)

    def wait_scatter():
        pltpu.make_async_copy(acc_ref, y_hbm.at[pl.ds(0, tm)], sem_s).wait()

    @pl.when(active & (f == 0))
    def _():
        @pl.when(t == 0)
        def _():
            start_gather(tok_ref)

        wait_gather()
        hb_ref[...] = _rms(xg_ref[...], g_ref[...]).astype(BF16)

        @pl.when(next_active)
        def _():
            start_gather(tokn_ref)

        @pl.when(t > 0)
        def _():
            wait_scatter()

        acc_ref[...] = jnp.zeros(acc_ref.shape, F32)

    @pl.when(active)
    def _():
        h = hb_ref[...]
        tf = w1_ref.shape[2]
        sub = MOE_SUB_FF if tf % MOE_SUB_FF == 0 else tf
        for c0 in range(0, tf, sub):
            a = _dot(h, w1_ref[0, :, c0:c0 + sub])
            b = _dot(h, w3_ref[0, :, c0:c0 + sub])
            u = (a * _sigmoid(a) * b).astype(BF16)
            acc_ref[...] += _dot(u, w2_ref[0, c0:c0 + sub, :])

    @pl.when(active & (f == last_f))
    def _():
        start_scatter()

        @pl.when(jnp.logical_not(next_active))
        def _():
            wait_scatter()


def _moe_experts(x, g, w1, w3, w2, tile_e, tile_nv, row_tok, row_dst, n_out_rows, tm, tf):
    n, d = x.shape
    dff = w1.shape[2]
    n_tiles = tile_e.shape[0]
    nf = dff // tf
    assert tm % ROW_DMA_BLOCK == 0

    def wcol(t, f, te, nv):
        return (te[t], 0, jnp.where(nv[t] > 0, f, nf - 1))

    def wrow(t, f, te, nv):
        return (te[t], jnp.where(nv[t] > 0, f, nf - 1), 0)

    grid_spec = pltpu.PrefetchScalarGridSpec(
        num_scalar_prefetch=2,
        grid=(n_tiles, nf),
        in_specs=[
            pl.BlockSpec((1, 1, tm), lambda t, f, te, nv: (t, 0, 0)),
            pl.BlockSpec((1, 1, tm), lambda t, f, te, nv: (jnp.minimum(t + 1, n_tiles - 1), 0, 0)),
            pl.BlockSpec((1, 1, tm), lambda t, f, te, nv: (t, 0, 0)),
            pl.BlockSpec(memory_space=pl.ANY),
            pl.BlockSpec((1, d), lambda t, f, te, nv: (0, 0)),
            pl.BlockSpec((1, d, tf), wcol),
            pl.BlockSpec((1, d, tf), wcol),
            pl.BlockSpec((1, tf, d), wrow),
        ],
        out_specs=pl.BlockSpec(memory_space=pl.ANY),
        scratch_shapes=[
            pltpu.VMEM((tm, d), F32),
            pltpu.VMEM((tm, d), BF16),
            pltpu.VMEM((tm, d), F32),
            pltpu.SMEM((1, tm), jnp.int32),
            pltpu.SMEM((1, tm), jnp.int32),
            pltpu.SemaphoreType.DMA,
            pltpu.SemaphoreType.DMA,
            pltpu.SemaphoreType.DMA,
        ],
    )
    return pl.pallas_call(
        functools.partial(_moe_kernel, tm=tm),
        grid_spec=grid_spec,
        out_shape=jax.ShapeDtypeStruct((n_out_rows, d), F32),
        compiler_params=_params(("arbitrary", "arbitrary")),
        name="moe_experts",
    )(tile_e, tile_nv, row_tok, row_tok, row_dst, x, g, w1, w3, w2)


def _moe_plan(eidx, n_exp, tm, n_pad):
    n = eidx.shape[0]
    rows = n * TOP_K
    flat_e = eidx.reshape(-1)
    order = jnp.argsort(flat_e, stable=True).astype(jnp.int32)
    sizes = jnp.bincount(flat_e, length=n_exp).astype(jnp.int32)
    starts = jnp.cumsum(sizes) - sizes
    tiles_per = (sizes + tm - 1) // tm
    tile_end = jnp.cumsum(tiles_per)
    tile_start = tile_end - tiles_per
    n_tiles = -(-rows // tm) + n_exp
    t = jnp.arange(n_tiles, dtype=jnp.int32)
    te = jnp.minimum(jnp.searchsorted(tile_end, t, side="right"), n_exp - 1).astype(jnp.int32)
    off = (t - tile_start[te]) * tm
    nv = jnp.where(t < tile_end[-1], jnp.clip(sizes[te] - off, 0, tm), 0).astype(jnp.int32)
    r = jnp.arange(tm, dtype=jnp.int32)[None, :]
    valid = r < nv[:, None]
    flat = order[jnp.clip(starts[te][:, None] + off[:, None] + r, 0, rows - 1)]
    tok = flat // TOP_K
    slot = flat - tok * TOP_K
    row_tok = jnp.where(valid, tok, 0)
    row_dst = jnp.where(valid, slot * n_pad + tok, n + r)
    return te, nv, row_tok[:, None, :], row_dst[:, None, :]


def _combine_kernel(x_ref, y0_ref, y1_ref, gate_ref, g_ref, o_ref, *, final):
    gate = gate_ref[...]
    x = x_ref[...] + (y0_ref[0] * gate[:, 0:1] + y1_ref[0] * gate[:, 1:2])
    o_ref[...] = _rms(x, g_ref[...]) if final else x


def _combine(x, y2, gates, g, tm, final, row0, n_rows):
    d = x.shape[1]
    assert row0 % tm == 0 and n_rows % tm == 0
    i0 = row0 // tm
    return pl.pallas_call(
        functools.partial(_combine_kernel, final=final),
        grid=(n_rows // tm,),
        in_specs=[
            pl.BlockSpec((tm, d), lambda i: (i0 + i, 0)),
            pl.BlockSpec((1, tm, d), lambda i: (0, i0 + i, 0)),
            pl.BlockSpec((1, tm, d), lambda i: (1, i0 + i, 0)),
            pl.BlockSpec((tm, LANES), lambda i: (i0 + i, 0)),
            pl.BlockSpec((1, d), lambda i: (0, 0)),
        ],
        out_specs=pl.BlockSpec((tm, d), lambda i: (i, 0)),
        out_shape=jax.ShapeDtypeStruct((n_rows, d), F32),
        compiler_params=_params(("parallel",)),
        name="combine",
    )(x, y2, y2, gates, g)


def _norm_kernel(x_ref, g_ref, o_ref):
    o_ref[...] = _rms(x_ref[...], g_ref[...])


def _norm(x, g, tm, row0, n_rows):
    d = x.shape[1]
    assert row0 % tm == 0 and n_rows % tm == 0
    i0 = row0 // tm
    return pl.pallas_call(
        _norm_kernel,
        grid=(n_rows // tm,),
        in_specs=[pl.BlockSpec((tm, d), lambda i: (i0 + i, 0)), pl.BlockSpec((1, d), lambda i: (0, 0))],
        out_specs=pl.BlockSpec((tm, d), lambda i: (i, 0)),
        out_shape=jax.ShapeDtypeStruct((n_rows, d), F32),
        compiler_params=_params(("parallel",)),
        name="final_norm",
    )(x, g)


def kernel(x_prompt, x_sample, cache_conv, state_gla, meta_tokens, norm_mix_g, w_in, w_gate_a2, b_gate_a,
           conv_w, conv_b, conv_ln_g, conv_ln_b, w_conv_out, gla_norm_g, w_gla_out, w_out, norm_ffn_g,
           ffn_w1, ffn_w3, ffn_w2, router_w, exp_w1, exp_w3, exp_w2, final_norm_g):
    bp, seq, d = x_prompt.shape
    bs, ts, _ = x_sample.shape
    n_meta = meta_tokens.shape[0]
    depth = w_in.shape[0]
    c_dim = conv_w.shape[2]
    width = conv_w.shape[1]
    n_heads, dkh, dvh = state_gla.shape[2:]
    dk, dv = n_heads * dkh, n_heads * dvh
    rank = w_gate_a2.shape[1]
    n_exp = router_w.shape[2]
    assert rank <= LANES and n_exp <= LANES

    assert n_meta == ts, "meta and sample blocks share the short-sequence calls"
    n_pm, n_mt, n_s = bp * seq, bp * n_meta, bs * ts
    row_short, row_s = n_pm, n_pm + n_mt
    n = n_pm + n_mt + n_s
    meta = jnp.broadcast_to(meta_tokens[None].astype(x_prompt.dtype), (bp, n_meta, d))
    x = x_prompt.reshape(n_pm, d)
    xs, xs_row0 = jnp.concatenate([meta.reshape(n_mt, d), x_sample.reshape(n_s, d)], axis=0), 0

    in_q = 2 * c_dim
    in_alr = in_q + 2 * dk + 2 * dv
    col_q, col_k, col_v, col_g = 0, dk, 2 * dk, 2 * dk + dv
    col_zc = col_g + dv
    col_zg = col_zc + d
    n_p = col_zg + d

    tm_big = _pick_tile(n, 1280, BF16_ROWS)
    tm_mid = _pick_tile(n, 640, BF16_ROWS)
    tn_in = _pick_tile(math.gcd(n_p, 2 * c_dim), 1024, 2 * LANES)
    tm_in = _pick_tile(math.gcd(seq, n_mt + n_s), 512, BF16_ROWS)
    tn_d = _pick_tile(d, 512, LANES)
    tt_gla = _pick_tile(seq, 512, CHUNK)
    tm_out_p = _pick_tile(n_pm, 640, BF16_ROWS)
    tm_out_s = _pick_tile(math.gcd(row_s, n_s), 640, BF16_ROWS)
    row = lambda v: v.reshape(1, -1).astype(F32)

    new_conv_p, new_conv_s, new_gla_p, new_gla_s = [], [], [], []
    for i in range(depth):
        if i:
            xs, xs_row0 = x, n_pm
        wi = w_in[i]
        half = tn_in // 2
        glu_cols = [wi[:, o + k * half:o + (k + 1) * half] for k in range(c_dim // half) for o in (0, c_dim)]
        wm = jnp.concatenate(glu_cols + [wi[:, in_q:in_alr], wi[:, in_alr + rank:]], axis=1).astype(BF16)
        wlr = jnp.pad(wi[:, in_alr:in_alr + rank], ((0, 0), (0, LANES - rank))).astype(BF16)
        wa2 = jnp.pad(w_gate_a2[i], ((0, LANES - rank), (0, 0))).astype(F32)
        w8 = jnp.broadcast_to(conv_w[i][:, None, :], (width, SUBLANES, c_dim)).astype(F32)
        cargs = (w8, row(conv_b[i]), row(conv_ln_g[i]), row(conv_ln_b[i]))
        p, lg, c_p, nc_p, glu_short = _inproj_conv(
            x, xs, xs_row0, row(norm_mix_g[i]), wm, wlr, wa2, row(b_gate_a[i]), *cargs,
            tm=tm_in, tn=tn_in, c_dim=c_dim, n_pm=n_pm, n_short=n_mt + n_s, seq=seq, n_meta=n_meta)
        c, nh_short = _conv_short(c_p, glu_short, cache_conv[i].astype(F32), *cargs, row0=row_short, tt=ts)
        nc_s = nh_short[bp:]

        gargs = dict(n=n, col_q=col_q, col_k=col_k, col_v=col_v, col_g=col_g, n_heads=n_heads, dkh=dkh, dvh=dvh)
        ng = row(gla_norm_g[i])
        og_short, st_short = _gla_branch(p, lg, ng, state_gla[i].astype(F32), nb=bp + bs, t_len=ts,
                                         row0=row_short, tt=ts, n_zero=bp, **gargs)
        og, ns_p = _gla_branch(p, lg, ng, st_short, nb=bp, t_len=seq, row0=0, tt=tt_gla, prev=og_short, **gargs)
        ns_s = st_short[bp:]

        mixed = _mix(c, og, p, w_conv_out[i].astype(BF16), w_gla_out[i].astype(BF16), col_zc, col_zg,
                     tm_big, tn_d)
        x = _resid_mm(mixed, w_out[i].astype(BF16), x, xs, xs_row0, n_pm, tm_in, d)

        j = i // 2
        last = i == depth - 1
        if i % 2 == 0:
            tf = _pick_tile(ffn_w1.shape[2], 512, LANES)
            x = _ffn(x, row(norm_ffn_g[i]), ffn_w1[j].astype(BF16), ffn_w3[j].astype(BF16),
                     ffn_w2[j].astype(BF16), tm_mid, tf)
            if last:
                y_p = _norm(x, row(final_norm_g), tm_out_p, 0, n_pm)
                y_s = _norm(x, row(final_norm_g), tm_out_s, row_s, n_s)
        else:
            rw = jnp.pad(router_w[j], ((0, 0), (0, LANES - n_exp))).astype(F32)
            eidx, gates = _router(x, row(norm_ffn_g[i]), rw, n_exp, tm_mid)
            tm_e = 1024 if n * TOP_K >= 8192 else 64
            n_pad = n + tm_e
            te, nv, row_tok, row_dst = _moe_plan(eidx[:, :TOP_K], n_exp, tm_e, n_pad)
            tf = _pick_tile(exp_w1.shape[3], 1024, LANES)
            y2 = _moe_experts(x, row(norm_ffn_g[i]), exp_w1[j].astype(BF16), exp_w3[j].astype(BF16),
                              exp_w2[j].astype(BF16), te, nv, row_tok, row_dst, TOP_K * n_pad, tm_e, tf)
            y2 = y2.reshape(TOP_K, n_pad, d)
            if last:
                y_p = _combine(x, y2, gates, row(final_norm_g), tm_out_p, True, 0, n_pm)
                y_s = _combine(x, y2, gates, row(final_norm_g), tm_out_s, True, row_s, n_s)
            else:
                x = _combine(x, y2, gates, row(final_norm_g), tm_mid, False, 0, n)
        new_conv_p.append(nc_p)
        new_conv_s.append(nc_s)
        new_gla_p.append(ns_p)
        new_gla_s.append(ns_s)

    return (y_p.reshape(bp, seq, d), y_s.reshape(bs, ts, d), jnp.stack(new_conv_p), jnp.stack(new_gla_p),
            jnp.stack(new_conv_s), jnp.stack(new_gla_s))
```

```python
import functools
import math

import jax
import jax.numpy as jnp
from jax import lax
from jax.experimental import pallas as pl
from jax.experimental.pallas import tpu as pltpu

CHUNK = 128
SUB = 16
GATE_TEMP = 16.0
EPS = 1e-6
TOP_K = 2
LANES = 128
SUBLANES = 8
BF16_ROWS = 16
VMEM_LIMIT_BYTES = 56 * 1024 * 1024
ROW_DMA_PRIORITY = 1
ROW_DMA_BLOCK = 32
GLA_HEAD_GROUP = 4
GLA_HALF = 64
MOE_SUB_FF = 512

F32 = jnp.float32
BF16 = jnp.bfloat16


def _pick_tile(n, target, mult):
    best = None
    for d in range(mult, min(n, target) + 1, mult):
        if n % d == 0:
            best = d
    assert best is not None, (n, target, mult)
    return best


def _params(sem):
    return pltpu.CompilerParams(dimension_semantics=sem, vmem_limit_bytes=VMEM_LIMIT_BYTES)


def _dot(a, b):
    return jnp.dot(a, b, preferred_element_type=F32)


def _split3(x):
    hi = x.astype(BF16)
    r1 = x - hi.astype(F32)
    mid = r1.astype(BF16)
    lo = (r1 - mid.astype(F32)).astype(BF16)
    return hi, mid, lo


def _dot_f32(a, b):
    a_hi, a_mid, _ = _split3(a)
    b_hi, b_mid, _ = _split3(b)
    return _dot(a_hi, b_hi) + _dot(a_mid, b_hi) + _dot(a_hi, b_mid)


def _rms(x, g):
    return x * lax.rsqrt(jnp.mean(x * x, axis=-1, keepdims=True) + EPS) * g


def _sigmoid(x):
    return jax.nn.sigmoid(x)


def _log_sigmoid(z):
    return jnp.minimum(z, 0.0) - jnp.log1p(jnp.exp(-jnp.abs(z)))


def _conv_taps(xs, w_ref, cols, off, width):
    return _conv_groups(xs, w_ref, cols, off, width, 1)[0]


def _conv_groups(xs, w_ref, cols, off, width, n_out):
    sub = lax.broadcasted_iota(jnp.int32, xs[0].shape, 0)
    totals = [None] * n_out
    for s in range(SUBLANES):
        taps = [(w, (off + w) // SUBLANES) for w in range(width) if (off + w) % SUBLANES == s]
        if not taps:
            continue
        shifted = {}
        for g in range(n_out):
            acc = None
            for w, a in taps:
                m = g + a
                if m not in shifted:
                    shifted[m] = xs[m] if s == 0 else jnp.where(sub >= s, xs[m], xs[m + 1])
                term = shifted[m] * w_ref[w, :, cols]
                acc = term if acc is None else acc + term
            if s:
                acc = pltpu.roll(acc, SUBLANES - s, 0)
            totals[g] = acc if totals[g] is None else totals[g] + acc
    return totals


def _ln_swish(y, cb, lng, lnb):
    y = y + cb
    mu = jnp.mean(y, axis=-1, keepdims=True)
    yc = y - mu
    c = yc * lax.rsqrt(jnp.mean(yc * yc, axis=-1, keepdims=True) + EPS) * lng + lnb
    return (c * _sigmoid(c)).astype(BF16)


def _inproj_conv_kernel(x_ref, xs_ref, g_ref, wm_ref, wlr_ref, wa2_ref, ba_ref, cw_ref, cb_ref, lng_ref, lnb_ref,
                        p_ref, lg_ref, c_ref, nh_ref, glu_hbm,
                        nb_ref, buf_ref, sg_ref, sem,
                        *, tm, n_glu, width, hp, n_short_tiles, tiles_per_seq, n_meta, units_per_step):
    i = pl.program_id(0)
    j = pl.program_id(1)
    hist = width - 1
    off = hp - hist
    half = buf_ref.shape[2]
    is_short = i < n_short_tiles
    ip = jnp.maximum(i - n_short_tiles, 0)
    seq_b = ip // tiles_per_seq
    tile_in_seq = ip - seq_b * tiles_per_seq
    n_groups = (off + width - 1) // SUBLANES + 2
    n_units = tm // BF16_ROWS
    n_conv_steps = -(-n_units // units_per_step)

    @pl.when(j == 0)
    def _():
        nb = _rms(jnp.where(is_short, xs_ref[...], x_ref[...]), g_ref[...]).astype(BF16)
        nb_ref[...] = nb
        a_lr = _dot(nb, wlr_ref[...])
        z = _dot_f32(a_lr, wa2_ref[...]) + ba_ref[...]
        lg_ref[...] = _log_sigmoid(z) * (1.0 / GATE_TEMP)

        @pl.when(jnp.logical_and(jnp.logical_not(is_short), tile_in_seq == 0))
        def _():
            r0 = pl.multiple_of(seq_b * n_meta, SUBLANES)
            for k in range(n_glu):
                buf_ref[k, 0:hp, :] = jnp.zeros((hp, half), F32)
                buf_ref[k, hp - n_meta:hp, :] = sg_ref[k, pl.ds(r0, n_meta), :]

        @pl.when(jnp.logical_and(jnp.logical_not(is_short), tile_in_seq > 0))
        def _():
            for k in range(n_glu):
                buf_ref[k, 0:hp, :] = buf_ref[k, tm:tm + hp, :]

    @pl.when(j < n_glu)
    def _():
        r = _dot(nb_ref[...], wm_ref[...])
        glu = r[:, :half] * _sigmoid(r[:, half:])

        @pl.when(is_short)
        def _():
            sg_ref[j, pl.ds(pl.multiple_of(i * tm, SUBLANES), tm), :] = glu

        @pl.when(jnp.logical_not(is_short))
        def _():
            buf_ref[j, hp:hp + tm, :] = glu

    @pl.when(jnp.logical_and(j == n_glu - 1, is_short))
    def _():
        rows = pl.ds(pl.multiple_of(i * tm, SUBLANES), tm)
        cp = pltpu.make_async_copy(sg_ref.at[:, rows], glu_hbm.at[:, rows], sem)
        cp.start()
        cp.wait()

    @pl.when(jnp.logical_and(j == n_glu, jnp.logical_and(jnp.logical_not(is_short),
                                                        tile_in_seq == tiles_per_seq - 1)))
    def _():
        for k in range(n_glu):
            nh_ref[0, :, k * half:(k + 1) * half] = buf_ref[k, tm + off:tm + hp, :]

    do_conv = jnp.logical_and(jnp.logical_not(is_short), j - n_glu < n_conv_steps)

    @pl.when(jnp.logical_and(j >= n_glu, do_conv))
    def _():
        p_ref[...] = _dot(nb_ref[...], wm_ref[...])
        groups_per_unit = BF16_ROWS // SUBLANES
        n_out = units_per_step * groups_per_unit
        base = pl.multiple_of((j - n_glu) * (units_per_step * BF16_ROWS), BF16_ROWS)
        ys = []
        for k in range(n_glu):
            win = [buf_ref[k, pl.ds(base + SUBLANES * g, SUBLANES), :] for g in range(n_out + n_groups - 2)]
            ys.append(_conv_groups(win, cw_ref, slice(k * half, (k + 1) * half), off, width, n_out))
        for u in range(units_per_step):
            y = jnp.concatenate([jnp.concatenate([ys[k][u * groups_per_unit + g2] for k in range(n_glu)], axis=1)
                                 for g2 in range(groups_per_unit)], axis=0)
            c_ref[pl.ds(base + u * BF16_ROWS, BF16_ROWS), :] = _ln_swish(y, cb_ref[...], lng_ref[...], lnb_ref[...])

    @pl.when(jnp.logical_and(j >= n_glu, jnp.logical_not(do_conv)))
    def _():
        p_ref[...] = _dot(nb_ref[...], wm_ref[...])


def _inproj_conv(x, xs, xs_row0, g, wm, wlr, wa2, ba, w8, cb, lng, lnb, *, tm, tn, c_dim, n_pm, n_short, seq,
                 n_meta):
    d = x.shape[1]
    n = n_pm + n_short
    half = tn // 2
    n_glu = c_dim // half
    n_p = wm.shape[1] - 2 * c_dim
    dk = wa2.shape[1]
    width = w8.shape[0]
    hp = -(-(width - 1) // SUBLANES) * SUBLANES
    assert xs_row0 % tm == 0 and x.shape[0] >= n_pm and xs.shape[0] >= xs_row0 + n_short
    assert c_dim % half == 0 and n_p % tn == 0 and n_pm % tm == 0 and n_short % tm == 0 and seq % tm == 0
    assert tm % BF16_ROWS == 0 and tm >= hp and n_meta <= width - 1 and n_meta % SUBLANES == 0
    ks, tps, bp = n_short // tm, seq // tm, n_pm // seq
    n_p_steps = n_p // tn
    units_per_step = -(-(tm // BF16_ROWS) // n_p_steps)
    assert (tm // BF16_ROWS) % units_per_step == 0 and tn % units_per_step == 0

    def rb(i):
        return jnp.where(i < ks, n_pm // tm + i, i - ks)

    vec = pl.BlockSpec((1, c_dim), lambda i, j: (0, 0))
    kern = functools.partial(_inproj_conv_kernel, tm=tm, n_glu=n_glu, width=width, hp=hp, n_short_tiles=ks,
                             tiles_per_seq=tps, n_meta=n_meta, units_per_step=units_per_step)
    return pl.pallas_call(
        kern,
        grid=(n // tm, n_glu + n_p_steps),
        in_specs=[
            pl.BlockSpec((tm, d), lambda i, j: (jnp.maximum(i - ks, 0), 0)),
            pl.BlockSpec((tm, d), lambda i, j: (xs_row0 // tm + jnp.minimum(i, ks - 1), 0)),
            pl.BlockSpec((1, d), lambda i, j: (0, 0)),
            pl.BlockSpec((d, tn), lambda i, j: (0, j)),
            pl.BlockSpec((d, LANES), lambda i, j: (0, 0)),
            pl.BlockSpec((LANES, dk), lambda i, j: (0, 0)),
            pl.BlockSpec((1, dk), lambda i, j: (0, 0)),
            pl.BlockSpec((width, SUBLANES, c_dim), lambda i, j: (0, 0, 0)), vec, vec, vec,
        ],
        out_specs=[
            pl.BlockSpec((tm, tn), lambda i, j: (rb(i), jnp.maximum(j - n_glu, 0))),
            pl.BlockSpec((tm, dk), lambda i, j: (rb(i), 0)),
            pl.BlockSpec((tm, c_dim), lambda i, j: (rb(i), 0)),
            pl.BlockSpec((1, width - 1, c_dim), lambda i, j: (jnp.clip((i - ks) // tps, 0, bp - 1), 0, 0)),
            pl.BlockSpec(memory_space=pl.ANY),
        ],
        out_shape=[jax.ShapeDtypeStruct((n, n_p), F32), jax.ShapeDtypeStruct((n, dk), F32),
                   jax.ShapeDtypeStruct((n, c_dim), BF16), jax.ShapeDtypeStruct((bp, width - 1, c_dim), F32),
                   jax.ShapeDtypeStruct((n_glu, n_short, half), F32)],
        scratch_shapes=[pltpu.VMEM((tm, d), BF16), pltpu.VMEM((n_glu, hp + tm, half), F32),
                        pltpu.VMEM((n_glu, n_short, half), F32), pltpu.SemaphoreType.DMA],
        compiler_params=_params(("arbitrary", "arbitrary")),
        name="inproj_conv",
    )(x, xs, g, wm, wlr, wa2, ba, w8, cb, lng, lnb)


def _conv_short_kernel(c_prev, glu_ref, hist_ref, w_ref, cb_ref, lng_ref, lnb_ref, c_ref, nh_ref, buf_ref, *,
                       tt, width, hp, n_zero):
    del c_prev
    hist = width - 1
    off = hp - hist
    n_glu, _, half = glu_ref.shape
    n_groups = (off + width - 1) // SUBLANES + 2
    for k in range(n_glu):
        buf_ref[k, 0:hp, :] = jnp.zeros((hp, half), F32)
        buf_ref[k, hp:hp + tt, :] = glu_ref[k]

    @pl.when(pl.program_id(0) >= n_zero)
    def _():
        for k in range(n_glu):
            buf_ref[k, off:hp, :] = hist_ref[0, :, k * half:(k + 1) * half]

    ys = []
    for r0 in range(0, tt, SUBLANES):
        parts = []
        for k in range(n_glu):
            xs = [buf_ref[k, r0 + SUBLANES * g:r0 + SUBLANES * (g + 1), :] for g in range(n_groups - 1)]
            parts.append(_conv_taps(xs, w_ref, slice(k * half, (k + 1) * half), off, width))
        ys.append(jnp.concatenate(parts, axis=1))
    c_ref[...] = _ln_swish(jnp.concatenate(ys, axis=0), cb_ref[...], lng_ref[...], lnb_ref[...])
    for k in range(n_glu):
        nh_ref[0, :, k * half:(k + 1) * half] = buf_ref[k, tt + off:tt + hp, :]


def _conv_short(c_prev, glu3, hist, w8, cb, lng, lnb, *, row0, tt):
    n, c_dim = c_prev.shape
    n_glu, n_short, half = glu3.shape
    width = w8.shape[0]
    hp = -(-(width - 1) // SUBLANES) * SUBLANES
    nb = n_short // tt
    n_zero = nb - hist.shape[0]
    assert row0 % tt == 0 and tt % BF16_ROWS == 0 and n_short % tt == 0 and n_zero >= 0
    rb0 = row0 // tt
    vec = pl.BlockSpec((1, c_dim), lambda s: (0, 0))
    return pl.pallas_call(
        functools.partial(_conv_short_kernel, tt=tt, width=width, hp=hp, n_zero=n_zero),
        grid=(nb,),
        in_specs=[
            pl.BlockSpec(memory_space=pl.ANY),
            pl.BlockSpec((n_glu, tt, half), lambda s: (0, s, 0)),
            pl.BlockSpec((1, width - 1, c_dim), lambda s: (jnp.maximum(s - n_zero, 0), 0, 0)),
            pl.BlockSpec((width, SUBLANES, c_dim), lambda s: (0, 0, 0)), vec, vec, vec,
        ],
        out_specs=[
            pl.BlockSpec((tt, c_dim), lambda s: (rb0 + s, 0)),
            pl.BlockSpec((1, width - 1, c_dim), lambda s: (s, 0, 0)),
        ],
        out_shape=[jax.ShapeDtypeStruct((n, c_dim), BF16),
                   jax.ShapeDtypeStruct((nb, width - 1, c_dim), F32)],
        scratch_shapes=[pltpu.VMEM((n_glu, hp + tt, half), F32)],
        input_output_aliases={0: 0},
        compiler_params=_params(("parallel",)),
        name="conv_short",
    )(c_prev, glu3, hist, w8, cb, lng, lnb)


def _gla_block(q, k, v, lg, s_ref, c):
    n_heads, dkh, dvh = s_ref.shape
    dk = q.shape[1]
    row = lax.broadcasted_iota(jnp.int32, (c, c), 0)
    col = lax.broadcasted_iota(jnp.int32, (c, c), 1)
    tri = jnp.where(row >= col, 1.0, 0.0).astype(BF16)
    lg3 = _split3(lg)
    b = _dot(tri, lg3[0]) + _dot(tri, lg3[1]) + _dot(tri, lg3[2])
    b_last = b[c - 1:c, :]
    vb = v.astype(BF16)
    qe = (q * jnp.exp(b)).astype(BF16)
    kd = (k * jnp.exp(b_last - b)).astype(BF16)
    tdims = (((0,), (0,)), ((), ()))
    decay = jnp.exp(jnp.transpose(jnp.broadcast_to(b_last, (SUBLANES, dk)))[:, 0:1])

    half = GLA_HALF if c % GLA_HALF == 0 else c
    qis, kis = [], []
    for i in range(c // SUB):
        r0 = i * SUB
        nk = r0 + SUB
        k0 = (r0 // half) * half
        ref = b[r0 - 1:r0, :] if i > 0 else jnp.zeros((1, dk), F32)
        qis.append((q[r0:nk] * jnp.exp(b[r0:nk] - ref)).astype(BF16))
        kis.append((k[k0:nk] * jnp.exp(ref - b[k0:nk])).astype(BF16))
    qxs, kxs = [], []
    for hb in range(1, c // half):
        h0 = hb * half
        ref = b[h0 - 1:h0, :]
        qxs.append((q[h0:h0 + half] * jnp.exp(b[h0:h0 + half] - ref)).astype(BF16))
        kxs.append((k[0:h0] * jnp.exp(ref - b[0:h0])).astype(BF16))

    kh = lambda x, h: x[:, h * dkh:(h + 1) * dkh]
    vh = lambda x, h: x[:, h * dvh:(h + 1) * dvh]
    heads = range(n_heads)
    o_state = [_dot(kh(qe, h), s_ref[h].astype(BF16)) for h in heads]
    kv = [lax.dot_general(kh(kd, h), vh(vb, h), tdims, preferred_element_type=F32) for h in heads]
    nt = (((1,), (1,)), ((), ()))
    a = [[lax.dot_general(kh(qis[i], h), kh(kis[i], h), nt, preferred_element_type=F32)
          for i in range(c // SUB)] for h in heads]
    ax = [[lax.dot_general(kh(qxs[j], h), kh(kxs[j], h), nt, preferred_element_type=F32)
           for j in range(len(qxs))] for h in heads]
    outs = []
    for h in heads:
        parts = []
        for i in range(c // SUB):
            r0 = i * SUB
            nk = r0 + SUB
            k0 = (r0 // half) * half
            causal = (lax.broadcasted_iota(jnp.int32, (SUB, nk - k0), 0) + (r0 - k0)
                      >= lax.broadcasted_iota(jnp.int32, (SUB, nk - k0), 1))
            parts.append(_dot(jnp.where(causal, a[h][i], 0.0).astype(BF16), vh(vb, h)[k0:nk]))
        o = o_state[h] + (jnp.concatenate(parts, axis=0) if len(parts) > 1 else parts[0])
        if qxs:
            cross = [jnp.zeros((half, dvh), F32)]
            cross += [_dot(ax[h][j].astype(BF16), vh(vb, h)[0:(j + 1) * half]) for j in range(len(qxs))]
            o = o + jnp.concatenate(cross, axis=0)
        outs.append(o)
    for h in heads:
        s_ref[h] = s_ref[h] * decay[h * dkh:(h + 1) * dkh] + kv[h]
    return outs


def _gla_kernel(*refs, tt, c, n_heads, scale, has_prev, n_zero):
    if has_prev:
        refs = refs[1:]
    q_ref, k_ref, v_ref, lg_ref, g_ref, ng_ref, s0_ref, o_ref, so_ref, s_ref = refs
    t = pl.program_id(1)
    dvh = s_ref.shape[2]

    @pl.when(t == 0)
    def _():
        s_ref[...] = s0_ref[0]

    @pl.when(jnp.logical_and(t == 0, pl.program_id(0) < n_zero))
    def _():
        s_ref[...] = jnp.zeros(s_ref.shape, F32)

    dkh = s_ref.shape[1]
    hg = GLA_HEAD_GROUP if n_heads % GLA_HEAD_GROUP == 0 else n_heads

    def block(r0):
        rows = pl.ds(r0, c)
        for h0 in range(0, n_heads, hg):
            kc = slice(h0 * dkh, (h0 + hg) * dkh)
            outs = _gla_block(q_ref[rows, kc] * scale, k_ref[rows, kc], v_ref[rows, h0 * dvh:(h0 + hg) * dvh],
                              lg_ref[rows, kc], s_ref.at[h0:h0 + hg], c)
            for h in range(hg):
                vc = slice((h0 + h) * dvh, (h0 + h + 1) * dvh)
                g = g_ref[rows, vc]
                o_ref[rows, vc] = (_rms(outs[h], ng_ref[...]) * (g * _sigmoid(g))).astype(BF16)

    if tt == c:
        block(0)
    else:
        def body(j, carry):
            block(pl.multiple_of(j * c, c))
            return carry

        lax.fori_loop(0, tt // c, body, 0)

    @pl.when(t == pl.num_programs(1) - 1)
    def _():
        so_ref[0] = s_ref[...]


def _gla_branch(p, lg, ng, s0, *, n, nb, t_len, row0, tt, col_q, col_k, col_v, col_g, n_heads, dkh, dvh,
                prev=None, n_zero=0):
    dk, dv = n_heads * dkh, n_heads * dvh
    c = min(CHUNK, t_len)
    assert t_len % tt == 0 and row0 % tt == 0 and tt % c == 0 and c % SUB == 0 and c % BF16_ROWS == 0
    nt = t_len // tt
    rb0 = row0 // tt
    assert col_q % dk == 0 and col_k % dk == 0 and col_v % dv == 0 and col_g % dv == 0
    cq, ck, cv, cg = col_q // dk, col_k // dk, col_v // dv, col_g // dv
    kern = functools.partial(_gla_kernel, tt=tt, c=c, n_heads=n_heads, scale=float(dkh) ** -0.5,
                             has_prev=prev is not None, n_zero=n_zero)
    in_specs = [pl.BlockSpec(memory_space=pl.ANY)] if prev is not None else []
    args = [prev] if prev is not None else []
    in_specs += [
        pl.BlockSpec((tt, dk), lambda b, t: (rb0 + b * nt + t, cq)),
        pl.BlockSpec((tt, dk), lambda b, t: (rb0 + b * nt + t, ck)),
        pl.BlockSpec((tt, dv), lambda b, t: (rb0 + b * nt + t, cv)),
        pl.BlockSpec((tt, dk), lambda b, t: (rb0 + b * nt + t, 0)),
        pl.BlockSpec((tt, dv), lambda b, t: (rb0 + b * nt + t, cg)),
        pl.BlockSpec((1, dvh), lambda b, t: (0, 0)),
        pl.BlockSpec((1, n_heads, dkh, dvh), lambda b, t: (jnp.maximum(b - n_zero, 0), 0, 0, 0)),
    ]
    args += [p, p, p, lg, p, ng, s0]
    return pl.pallas_call(
        kern,
        grid=(nb, nt),
        in_specs=in_specs,
        out_specs=[
            pl.BlockSpec((tt, dv), lambda b, t: (rb0 + b * nt + t, 0)),
            pl.BlockSpec((1, n_heads, dkh, dvh), lambda b, t: (b, 0, 0, 0)),
        ],
        out_shape=[jax.ShapeDtypeStruct((n, dv), BF16),
                   jax.ShapeDtypeStruct((nb, n_heads, dkh, dvh), F32)],
        scratch_shapes=[pltpu.VMEM((n_heads, dkh, dvh), F32)],
        input_output_aliases={0: 0} if prev is not None else {},
        compiler_params=_params(("parallel", "arbitrary")),
        name="gla_branch",
    )(*args)


def _mix_kernel(c_ref, og_ref, zc_ref, zg_ref, wc_ref, wg_ref, o_ref):
    a = _dot(c_ref[...], wc_ref[...])
    b = _dot(og_ref[...], wg_ref[...])
    o_ref[...] = (_sigmoid(zc_ref[...]) * a + _sigmoid(zg_ref[...]) * b).astype(BF16)


def _mix(c, og, p, wc, wg, col_zc, col_zg, tm, tn):
    n, d = c.shape
    d_out = wc.shape[1]
    assert col_zc % tn == 0 and col_zg % tn == 0
    jc, jg = col_zc // tn, col_zg // tn
    return pl.pallas_call(
        _mix_kernel,
        grid=(n // tm, d_out // tn),
        in_specs=[
            pl.BlockSpec((tm, d), lambda i, j: (i, 0)),
            pl.BlockSpec((tm, og.shape[1]), lambda i, j: (i, 0)),
            pl.BlockSpec((tm, tn), lambda i, j: (i, jc + j)),
            pl.BlockSpec((tm, tn), lambda i, j: (i, jg + j)),
            pl.BlockSpec((d, tn), lambda i, j: (0, j)),
            pl.BlockSpec((og.shape[1], tn), lambda i, j: (0, j)),
        ],
        out_specs=pl.BlockSpec((tm, tn), lambda i, j: (i, j)),
        out_shape=jax.ShapeDtypeStruct((n, d_out), BF16),
        compiler_params=_params(("parallel", "arbitrary")),
        name="mix",
    )(c, og, p, p, wc, wg)


def _resid_mm_kernel(a_ref, w_ref, x_ref, xs_ref, o_ref, *, n_first):
    x = jnp.where(pl.program_id(0) < n_first, x_ref[...], xs_ref[...])
    o_ref[...] = x + _dot(a_ref[...], w_ref[...])


def _resid_mm(a, w, x, xs, xs_row0, n_pm, tm, tn):
    n, k = a.shape
    d_out = w.shape[1]
    assert n_pm % tm == 0 and xs_row0 % tm == 0 and n % tm == 0
    n_first = n_pm // tm
    return pl.pallas_call(
        functools.partial(_resid_mm_kernel, n_first=n_first),
        grid=(n // tm, d_out // tn),
        in_specs=[
            pl.BlockSpec((tm, k), lambda i, j: (i, 0)),
            pl.BlockSpec((k, tn), lambda i, j: (0, j)),
            pl.BlockSpec((tm, tn), lambda i, j: (jnp.minimum(i, n_first - 1), j)),
            pl.BlockSpec((tm, tn), lambda i, j: (xs_row0 // tm + jnp.maximum(i - n_first, 0), j)),
        ],
        out_specs=pl.BlockSpec((tm, tn), lambda i, j: (i, j)),
        out_shape=jax.ShapeDtypeStruct((n, d_out), F32),
        compiler_params=_params(("parallel", "arbitrary")),
        name="out_proj",
    )(a, w, x, xs)


def _ffn_kernel(x_ref, g_ref, w1_ref, w3_ref, w2_ref, o_ref, hb_ref):
    f = pl.program_id(1)

    @pl.when(f == 0)
    def _():
        x = x_ref[...]
        hb_ref[...] = _rms(x, g_ref[...]).astype(BF16)
        o_ref[...] = x

    h = hb_ref[...]
    a = _dot(h, w1_ref[...])
    b = _dot(h, w3_ref[...])
    t = (a * _sigmoid(a) * b).astype(BF16)
    o_ref[...] += _dot(t, w2_ref[...])


def _ffn(x, g, w1, w3, w2, tm, tf):
    n, d = x.shape
    dff = w1.shape[1]
    return pl.pallas_call(
        _ffn_kernel,
        grid=(n // tm, dff // tf),
        in_specs=[
            pl.BlockSpec((tm, d), lambda i, f: (i, 0)),
            pl.BlockSpec((1, d), lambda i, f: (0, 0)),
            pl.BlockSpec((d, tf), lambda i, f: (0, f)),
            pl.BlockSpec((d, tf), lambda i, f: (0, f)),
            pl.BlockSpec((tf, d), lambda i, f: (f, 0)),
        ],
        out_specs=pl.BlockSpec((tm, d), lambda i, f: (i, 0)),
        out_shape=jax.ShapeDtypeStruct((n, d), F32),
        scratch_shapes=[pltpu.VMEM((tm, d), BF16)],
        compiler_params=_params(("parallel", "arbitrary")),
        name="ffn",
    )(x, g, w1, w3, w2)


def _router_kernel(x_ref, g_ref, rw_ref, idx_ref, gate_ref, *, n_exp):
    h = _rms(x_ref[...], g_ref[...])
    logits = _dot_f32(h, rw_ref[...])
    lane = lax.broadcasted_iota(jnp.int32, logits.shape, 1).astype(F32)
    neg = jnp.float32(-jnp.inf)
    l1 = jnp.where(lane < n_exp, logits, neg)
    m1 = jnp.max(l1, axis=-1, keepdims=True)
    i1 = jnp.min(jnp.where(l1 == m1, lane, float(LANES)), axis=-1, keepdims=True)
    l2 = jnp.where(lane == i1, neg, l1)
    m2 = jnp.max(l2, axis=-1, keepdims=True)
    i2 = jnp.min(jnp.where(l2 == m2, lane, float(LANES)), axis=-1, keepdims=True)
    e = jnp.exp(m2 - m1)
    den = 1.0 + e
    idx_ref[...] = jnp.where(lane == 0, i1, jnp.where(lane == 1, i2, 0.0)).astype(jnp.int32)
    gate_ref[...] = jnp.where(lane == 0, 1.0 / den, jnp.where(lane == 1, e / den, 0.0))


def _router(x, g, rw, n_exp, tm):
    n, d = x.shape
    return pl.pallas_call(
        functools.partial(_router_kernel, n_exp=n_exp),
        grid=(n // tm,),
        in_specs=[
            pl.BlockSpec((tm, d), lambda i: (i, 0)),
            pl.BlockSpec((1, d), lambda i: (0, 0)),
            pl.BlockSpec((d, LANES), lambda i: (0, 0)),
        ],
        out_specs=[pl.BlockSpec((tm, LANES), lambda i: (i, 0)), pl.BlockSpec((tm, LANES), lambda i: (i, 0))],
        out_shape=[jax.ShapeDtypeStruct((n, LANES), jnp.int32), jax.ShapeDtypeStruct((n, LANES), F32)],
        compiler_params=_params(("parallel",)),
        name="router",
    )(x, g, rw)


def _moe_kernel(te_ref, nv_ref, tok_ref, tokn_ref, dst_ref, x_hbm, g_ref, w1_ref, w3_ref, w2_ref, y_hbm,
                xg_ref, hb_ref, acc_ref, tok_s, dst_s, sem_g, sem_s, sem_i, *, tm):
    t = pl.program_id(0)
    f = pl.program_id(1)
    n_tiles = pl.num_programs(0)
    last_f = pl.num_programs(1) - 1
    active = nv_ref[t] > 0
    next_active = jnp.logical_and(t + 1 < n_tiles, nv_ref[jnp.minimum(t + 1, n_tiles - 1)] > 0)

    def start_gather(idx_ref):
        cp = pltpu.make_async_copy(idx_ref.at[0], tok_s, sem_i)
        cp.start()
        cp.wait()

        def issue(blk, carry):
            r0 = pl.multiple_of(blk * ROW_DMA_BLOCK, ROW_DMA_BLOCK)
            dst = xg_ref.at[pl.ds(r0, ROW_DMA_BLOCK)]
            for r in range(ROW_DMA_BLOCK):
                pltpu.make_async_copy(x_hbm.at[pl.ds(tok_s[0, r0 + r], 1)], dst.at[pl.ds(r, 1)], sem_g).start(
                    priority=ROW_DMA_PRIORITY)
            return carry

        lax.fori_loop(0, tm // ROW_DMA_BLOCK, issue, 0)

    def wait_gather():
        pltpu.make_async_copy(x_hbm.at[pl.ds(0, tm)], xg_ref, sem_g).wait()

    def start_scatter():
        cp = pltpu.make_async_copy(dst_ref.at[0], dst_s, sem_i)
        cp.start()
        cp.wait()

        def issue(blk, carry):
            r0 = pl.multiple_of(blk * ROW_DMA_BLOCK, ROW_DMA_BLOCK)
            src = acc_ref.at[pl.ds(r0, ROW_DMA_BLOCK)]
            for r in range(ROW_DMA_BLOCK):
                pltpu.make_async_copy(src.at[pl.ds(r, 1)], y_hbm.at[pl.ds(dst_s[0, r0 + r], 1)], sem_s).start(
                    priority=ROW_DMA_PRIORITY)
            return carry

        lax.fori_loop(0, tm // ROW_DMA_BLOCK, issue, 0)

    def wait_scatter():
        pltpu.make_async_copy(acc_ref, y_hbm.at[pl.ds(0, tm)], sem_s).wait()

    @pl.when(active & (f == 0))
    def _():
        @pl.when(t == 0)
        def _():
            start_gather(tok_ref)

        wait_gather()
        hb_ref[...] = _rms(xg_ref[...], g_ref[...]).astype(BF16)

        @pl.when(next_active)
        def _():
            start_gather(tokn_ref)

        @pl.when(t > 0)
        def _():
            wait_scatter()

        acc_ref[...] = jnp.zeros(acc_ref.shape, F32)

    @pl.when(active)
    def _():
        h = hb_ref[...]
        tf = w1_ref.shape[2]
        sub = MOE_SUB_FF if tf % MOE_SUB_FF == 0 else tf
        for c0 in range(0, tf, sub):
            a = _dot(h, w1_ref[0, :, c0:c0 + sub])
            b = _dot(h, w3_ref[0, :, c0:c0 + sub])
            u = (a * _sigmoid(a) * b).astype(BF16)
            acc_ref[...] += _dot(u, w2_ref[0, c0:c0 + sub, :])

    @pl.when(active & (f == last_f))
    def _():
        start_scatter()

        @pl.when(jnp.logical_not(next_active))
        def _():
            wait_scatter()


def _moe_experts(x, g, w1, w3, w2, tile_e, tile_nv, row_tok, row_dst, n_out_rows, tm, tf):
    n, d = x.shape
    dff = w1.shape[2]
    n_tiles = tile_e.shape[0]
    nf = dff // tf
    assert tm % ROW_DMA_BLOCK == 0

    def wcol(t, f, te, nv):
        return (te[t], 0, jnp.where(nv[t] > 0, f, nf - 1))

    def wrow(t, f, te, nv):
        return (te[t], jnp.where(nv[t] > 0, f, nf - 1), 0)

    grid_spec = pltpu.PrefetchScalarGridSpec(
        num_scalar_prefetch=2,
        grid=(n_tiles, nf),
        in_specs=[
            pl.BlockSpec((1, 1, tm), lambda t, f, te, nv: (t, 0, 0)),
            pl.BlockSpec((1, 1, tm), lambda t, f, te, nv: (jnp.minimum(t + 1, n_tiles - 1), 0, 0)),
            pl.BlockSpec((1, 1, tm), lambda t, f, te, nv: (t, 0, 0)),
            pl.BlockSpec(memory_space=pl.ANY),
            pl.BlockSpec((1, d), lambda t, f, te, nv: (0, 0)),
            pl.BlockSpec((1, d, tf), wcol),
            pl.BlockSpec((1, d, tf), wcol),
            pl.BlockSpec((1, tf, d), wrow),
        ],
        out_specs=pl.BlockSpec(memory_space=pl.ANY),
        scratch_shapes=[
            pltpu.VMEM((tm, d), F32),
            pltpu.VMEM((tm, d), BF16),
            pltpu.VMEM((tm, d), F32),
            pltpu.SMEM((1, tm), jnp.int32),
            pltpu.SMEM((1, tm), jnp.int32),
            pltpu.SemaphoreType.DMA,
            pltpu.SemaphoreType.DMA,
            pltpu.SemaphoreType.DMA,
        ],
    )
    return pl.pallas_call(
        functools.partial(_moe_kernel, tm=tm),
        grid_spec=grid_spec,
        out_shape=jax.ShapeDtypeStruct((n_out_rows, d), F32),
        compiler_params=_params(("arbitrary", "arbitrary")),
        name="moe_experts",
    )(tile_e, tile_nv, row_tok, row_tok, row_dst, x, g, w1, w3, w2)


def _moe_plan(eidx, n_exp, tm, n_pad):
    n = eidx.shape[0]
    rows = n * TOP_K
    flat_e = eidx.reshape(-1)
    order = jnp.argsort(flat_e, stable=True).astype(jnp.int32)
    sizes = jnp.bincount(flat_e, length=n_exp).astype(jnp.int32)
    starts = jnp.cumsum(sizes) - sizes
    tiles_per = (sizes + tm - 1) // tm
    tile_end = jnp.cumsum(tiles_per)
    tile_start = tile_end - tiles_per
    n_tiles = -(-rows // tm) + n_exp
    t = jnp.arange(n_tiles, dtype=jnp.int32)
    te = jnp.minimum(jnp.searchsorted(tile_end, t, side="right"), n_exp - 1).astype(jnp.int32)
    off = (t - tile_start[te]) * tm
    nv = jnp.where(t < tile_end[-1], jnp.clip(sizes[te] - off, 0, tm), 0).astype(jnp.int32)
    r = jnp.arange(tm, dtype=jnp.int32)[None, :]
    valid = r < nv[:, None]
    flat = order[jnp.clip(starts[te][:, None] + off[:, None] + r, 0, rows - 1)]
    tok = flat // TOP_K
    slot = flat - tok * TOP_K
    row_tok = jnp.where(valid, tok, 0)
    row_dst = jnp.where(valid, slot * n_pad + tok, n + r)
    return te, nv, row_tok[:, None, :], row_dst[:, None, :]


def _combine_kernel(x_ref, y0_ref, y1_ref, gate_ref, g_ref, o_ref, *, final):
    gate = gate_ref[...]
    x = x_ref[...] + (y0_ref[0] * gate[:, 0:1] + y1_ref[0] * gate[:, 1:2])
    o_ref[...] = _rms(x, g_ref[...]) if final else x


def _combine(x, y2, gates, g, tm, final, row0, n_rows):
    d = x.shape[1]
    assert row0 % tm == 0 and n_rows % tm == 0
    i0 = row0 // tm
    return pl.pallas_call(
        functools.partial(_combine_kernel, final=final),
        grid=(n_rows // tm,),
        in_specs=[
            pl.BlockSpec((tm, d), lambda i: (i0 + i, 0)),
            pl.BlockSpec((1, tm, d), lambda i: (0, i0 + i, 0)),
            pl.BlockSpec((1, tm, d), lambda i: (1, i0 + i, 0)),
            pl.BlockSpec((tm, LANES), lambda i: (i0 + i, 0)),
            pl.BlockSpec((1, d), lambda i: (0, 0)),
        ],
        out_specs=pl.BlockSpec((tm, d), lambda i: (i, 0)),
        out_shape=jax.ShapeDtypeStruct((n_rows, d), F32),
        compiler_params=_params(("parallel",)),
        name="combine",
    )(x, y2, y2, gates, g)


def _norm_kernel(x_ref, g_ref, o_ref):
    o_ref[...] = _rms(x_ref[...], g_ref[...])


def _norm(x, g, tm, row0, n_rows):
    d = x.shape[1]
    assert row0 % tm == 0 and n_rows % tm == 0
    i0 = row0 // tm
    return pl.pallas_call(
        _norm_kernel,
        grid=(n_rows // tm,),
        in_specs=[pl.BlockSpec((tm, d), lambda i: (i0 + i, 0)), pl.BlockSpec((1, d), lambda i: (0, 0))],
        out_specs=pl.BlockSpec((tm, d), lambda i: (i, 0)),
        out_shape=jax.ShapeDtypeStruct((n_rows, d), F32),
        compiler_params=_params(("parallel",)),
        name="final_norm",
    )(x, g)


def kernel(x_prompt, x_sample, cache_conv, state_gla, meta_tokens, norm_mix_g, w_in, w_gate_a2, b_gate_a,
           conv_w, conv_b, conv_ln_g, conv_ln_b, w_conv_out, gla_norm_g, w_gla_out, w_out, norm_ffn_g,
           ffn_w1, ffn_w3, ffn_w2, router_w, exp_w1, exp_w3, exp_w2, final_norm_g):
    bp, seq, d = x_prompt.shape
    bs, ts, _ = x_sample.shape
    n_meta = meta_tokens.shape[0]
    depth = w_in.shape[0]
    c_dim = conv_w.shape[2]
    width = conv_w.shape[1]
    n_heads, dkh, dvh = state_gla.shape[2:]
    dk, dv = n_heads * dkh, n_heads * dvh
    rank = w_gate_a2.shape[1]
    n_exp = router_w.shape[2]
    assert rank <= LANES and n_exp <= LANES

    assert n_meta == ts, "meta and sample blocks share the short-sequence calls"
    n_pm, n_mt, n_s = bp * seq, bp * n_meta, bs * ts
    row_short, row_s = n_pm, n_pm + n_mt
    n = n_pm + n_mt + n_s
    meta = jnp.broadcast_to(meta_tokens[None].astype(x_prompt.dtype), (bp, n_meta, d))
    x = x_prompt.reshape(n_pm, d)
    xs, xs_row0 = jnp.concatenate([meta.reshape(n_mt, d), x_sample.reshape(n_s, d)], axis=0), 0

    in_q = 2 * c_dim
    in_alr = in_q + 2 * dk + 2 * dv
    col_q, col_k, col_v, col_g = 0, dk, 2 * dk, 2 * dk + dv
    col_zc = col_g + dv
    col_zg = col_zc + d
    n_p = col_zg + d

    tm_big = _pick_tile(n, 1280, BF16_ROWS)
    tm_mid = _pick_tile(n, 640, BF16_ROWS)
    tn_in = _pick_tile(math.gcd(n_p, 2 * c_dim), 1024, 2 * LANES)
    tm_in = _pick_tile(math.gcd(seq, n_mt + n_s), 512, BF16_ROWS)
    tn_d = _pick_tile(d, 512, LANES)
    tt_gla = _pick_tile(seq, 512, CHUNK)
    tm_out_p = _pick_tile(n_pm, 640, BF16_ROWS)
    tm_out_s = _pick_tile(math.gcd(row_s, n_s), 640, BF16_ROWS)
    row = lambda v: v.reshape(1, -1).astype(F32)

    new_conv_p, new_conv_s, new_gla_p, new_gla_s = [], [], [], []
    for i in range(depth):
        if i:
            xs, xs_row0 = x, n_pm
        wi = w_in[i]
        half = tn_in // 2
        glu_cols = [wi[:, o + k * half:o + (k + 1) * half] for k in range(c_dim // half) for o in (0, c_dim)]
        wm = jnp.concatenate(glu_cols + [wi[:, in_q:in_alr], wi[:, in_alr + rank:]], axis=1).astype(BF16)
        wlr = jnp.pad(wi[:, in_alr:in_alr + rank], ((0, 0), (0, LANES - rank))).astype(BF16)
        wa2 = jnp.pad(w_gate_a2[i], ((0, LANES - rank), (0, 0))).astype(F32)
        w8 = jnp.broadcast_to(conv_w[i][:, None, :], (width, SUBLANES, c_dim)).astype(F32)
        cargs = (w8, row(conv_b[i]), row(conv_ln_g[i]), row(conv_ln_b[i]))
        p, lg, c_p, nc_p, glu_short = _inproj_conv(
            x, xs, xs_row0, row(norm_mix_g[i]), wm, wlr, wa2, row(b_gate_a[i]), *cargs,
            tm=tm_in, tn=tn_in, c_dim=c_dim, n_pm=n_pm, n_short=n_mt + n_s, seq=seq, n_meta=n_meta)
        c, nh_short = _conv_short(c_p, glu_short, cache_conv[i].astype(F32), *cargs, row0=row_short, tt=ts)
        nc_s = nh_short[bp:]

        gargs = dict(n=n, col_q=col_q, col_k=col_k, col_v=col_v, col_g=col_g, n_heads=n_heads, dkh=dkh, dvh=dvh)
        ng = row(gla_norm_g[i])
        og_short, st_short = _gla_branch(p, lg, ng, state_gla[i].astype(F32), nb=bp + bs, t_len=ts,
                                         row0=row_short, tt=ts, n_zero=bp, **gargs)
        og, ns_p = _gla_branch(p, lg, ng, st_short, nb=bp, t_len=seq, row0=0, tt=tt_gla, prev=og_short, **gargs)
        ns_s = st_short[bp:]

        mixed = _mix(c, og, p, w_conv_out[i].astype(BF16), w_gla_out[i].astype(BF16), col_zc, col_zg,
                     tm_big, tn_d)
        x = _resid_mm(mixed, w_out[i].astype(BF16), x, xs, xs_row0, n_pm, tm_in, d)

        j = i // 2
        last = i == depth - 1
        if i % 2 == 0:
            tf = _pick_tile(ffn_w1.shape[2], 512, LANES)
            x = _ffn(x, row(norm_ffn_g[i]), ffn_w1[j].astype(BF16), ffn_w3[j].astype(BF16),
                     ffn_w2[j].astype(BF16), tm_mid, tf)
            if last:
                y_p = _norm(x, row(final_norm_g), tm_out_p, 0, n_pm)
                y_s = _norm(x, row(final_norm_g), tm_out_s, row_s, n_s)
        else:
            rw = jnp.pad(router_w[j], ((0, 0), (0, LANES - n_exp))).astype(F32)
            eidx, gates = _router(x, row(norm_ffn_g[i]), rw, n_exp, tm_mid)
            tm_e = 1024 if n * TOP_K >= 8192 else 64
            n_pad = n + tm_e
            te, nv, row_tok, row_dst = _moe_plan(eidx[:, :TOP_K], n_exp, tm_e, n_pad)
            tf = _pick_tile(exp_w1.shape[3], 1024, LANES)
            y2 = _moe_experts(x, row(norm_ffn_g[i]), exp_w1[j].astype(BF16), exp_w3[j].astype(BF16),
                              exp_w2[j].astype(BF16), te, nv, row_tok, row_dst, TOP_K * n_pad, tm_e, tf)
            y2 = y2.reshape(TOP_K, n_pad, d)
            if last:
                y_p = _combine(x, y2, gates, row(final_norm_g), tm_out_p, True, 0, n_pm)
                y_s = _combine(x, y2, gates, row(final_norm_g), tm_out_s, True, row_s, n_s)
            else:
                x = _combine(x, y2, gates, row(final_norm_g), tm_mid, False, 0, n)
        new_conv_p.append(nc_p)
        new_conv_s.append(nc_s)
        new_gla_p.append(ns_p)
        new_gla_s.append(ns_s)

    return (y_p.reshape(bp, seq, d), y_s.reshape(bs, ts, d), jnp.stack(new_conv_p), jnp.stack(new_gla_p),
            jnp.stack(new_conv_s), jnp.stack(new_gla_s))
```
